```python
import jax
import jax.numpy as jnp
from jax import lax
import numpy as np

D_MODEL = 2048
BATCH = 1
SEQ = 16384
DEPTH = 2

GRID_W = 64
CTX_LEN = 256
NORM_EPS = 1e-6

RWKV_WIDTH = D_MODEL // 2
RWKV_HEAD = 64
RWKV_HEADS = RWKV_WIDTH // RWKV_HEAD
DECAY_LORA = 64
ICLR_LORA = 64
GATE_LORA = 160
LNX_EPS = 64e-5
RWKV_COLS = (RWKV_WIDTH, RWKV_WIDTH, RWKV_WIDTH, DECAY_LORA, DECAY_LORA, ICLR_LORA, ICLR_LORA, GATE_LORA)
RWKV_IN = 3 * RWKV_WIDTH + 2 * DECAY_LORA + 2 * ICLR_LORA + GATE_LORA

RET_WIDTH = D_MODEL - RWKV_WIDTH
RET_HEAD = 128
RET_HEADS = RET_WIDTH // RET_HEAD
RET_CHUNK = 128
ROPE_BASE = 10000.0
RET_IN = 4 * RET_WIDTH
P_IN = RWKV_IN + RET_IN

N_EXPERTS = 32
N_GROUPS = 4
EXPERTS_PER_GROUP = N_EXPERTS // N_GROUPS
TOP_K = 2
D_EXPERT = 1024
MOE_BLOCK = 128

kernel_name = "hybrid_rwkv7_retention_moe_diffusion"


def rms_norm(x, g):
    xf = x.astype(jnp.float32)
    y = xf * lax.rsqrt(jnp.mean(xf * xf, axis=-1, keepdims=True) + NORM_EPS)
    return (y * g.astype(jnp.float32)).astype(x.dtype)


def split_cols(t, sizes):
    return jnp.split(t, [int(s) for s in np.cumsum(sizes)[:-1]], axis=-1)


def shift_seq(p, mu_prev, mu_next):
    prev = jnp.pad(p[:, :-1], ((0, 0), (1, 0), (0, 0)))
    nxt = jnp.pad(p[:, 1:], ((0, 0), (0, 1), (0, 0)))
    return p + mu_prev * (prev - p) + mu_next * (nxt - p)


def shift_grid(p, mu, rows):
    b, n, ch = p.shape
    g = p.reshape(b, rows, GRID_W, ch)
    left = jnp.pad(g[:, :, :-1], ((0, 0), (0, 0), (1, 0), (0, 0)))
    right = jnp.pad(g[:, :, 1:], ((0, 0), (0, 0), (0, 1), (0, 0)))
    up = jnp.pad(g[:, :-1], ((0, 0), (1, 0), (0, 0), (0, 0)))
    down = jnp.pad(g[:, 1:], ((0, 0), (0, 1), (0, 0), (0, 0)))
    out = g + mu[0] * (left - g) + mu[1] * (right - g) + mu[2] * (up - g) + mu[3] * (down - g)
    return out.reshape(b, n, ch)


def rwkv7_inputs(u, w0, w2, a0, a2, g2, k_k, k_a):
    b, n, _ = u.shape
    r, k, v, wd_f, wd_b, ad_f, ad_b, gd = split_cols(u.astype(jnp.float32), RWKV_COLS)

    def heads(t):
        return t.reshape(b, n, RWKV_HEADS, RWKV_HEAD)

    kk = heads(k * k_k)
    kk = kk / jnp.maximum(jnp.linalg.norm(kk, axis=-1, keepdims=True), 1e-12)
    per_dir = []
    for d, (wd, ad) in enumerate(((wd_f, ad_f), (wd_b, ad_b))):
        w_log = -jax.nn.softplus(-(w0[d] + jnp.tanh(wd) @ w2[d])) - 0.5
        a = jax.nn.sigmoid(a0[d] + ad @ a2[d])
        decay = jnp.exp(-jnp.exp(w_log))
        per_dir.append((heads(decay), heads(k * (1.0 + (a - 1.0) * k_a)), heads(a)))
    g = jax.nn.sigmoid(gd) @ g2
    return heads(r), heads(v), kk, g, per_dir


def rwkv7_scan(r, w, k, v, kk, a, s0, reverse):
    def step(s, inp):
        r_t, w_t, k_t, v_t, kk_t, a_t = inp
        sa = jnp.einsum('bhvk,bhk->bhv', s, kk_t)
        s = (s * w_t[:, :, None, :] - sa[..., None] * (kk_t * a_t)[:, :, None, :]
             + v_t[..., None] * k_t[:, :, None, :])
        return s, jnp.einsum('bhvk,bhk->bhv', s, r_t)

    xs = tuple(jnp.swapaxes(t, 0, 1) for t in (r, w, k, v, kk, a))
    s_fin, ys = lax.scan(step, s0, xs, reverse=reverse)
    return jnp.swapaxes(ys, 0, 1), s_fin


def rwkv7_output(y, r, v, g, per_dir, r_k, lnx_w, lnx_b):
    b, n = y.shape[:2]
    mu = jnp.mean(y, axis=-1, keepdims=True)
    var = jnp.mean(jnp.square(y - mu), axis=-1, keepdims=True)
    yn = (y - mu) * lax.rsqrt(var + LNX_EPS)
    k_f, k_b = per_dir[0][1], per_dir[1][1]
    bonus = (jnp.sum(r * k_f * r_k, axis=-1, keepdims=True)
             + jnp.sum(r * k_b * r_k, axis=-1, keepdims=True)) * v
    return (yn.reshape(b, n, RWKV_WIDTH) * lnx_w + lnx_b + bonus.reshape(b, n, RWKV_WIDTH)) * g


def rwkv7_mixer(u_c, u_x, w0, w2, a0, a2, g2, k_k, k_a, r_k, lnx_w, lnx_b, with_ctx):
    r_c, v_c, kk_c, g_c, dirs_c = rwkv7_inputs(u_c, w0, w2, a0, a2, g2, k_k, k_a)
    r_x, v_x, kk_x, g_x, dirs_x = rwkv7_inputs(u_x, w0, w2, a0, a2, g2, k_k, k_a)
    s0 = jnp.zeros((u_x.shape[0], RWKV_HEADS, RWKV_HEAD, RWKV_HEAD), jnp.float32)
    ys_c, ys_x = [], []
    for d in range(2):
        rev = d == 1
        dec_c, kd_c, a_c = dirs_c[d]
        y_c, s_c = rwkv7_scan(r_c, dec_c, kd_c, v_c, kk_c, a_c, s0, rev)
        dec_x, kd_x, a_x = dirs_x[d]
        y_x, _ = rwkv7_scan(r_x, dec_x, kd_x, v_x, kk_x, a_x, s_c, rev)
        ys_c.append(y_c)
        ys_x.append(y_x)
    out_x = rwkv7_output(ys_x[0] + ys_x[1], r_x, v_x, g_x, dirs_x, r_k, lnx_w, lnx_b)
    out_c = rwkv7_output(ys_c[0] + ys_c[1], r_c, v_c, g_c, dirs_c, r_k, lnx_w, lnx_b) if with_ctx else None
    return out_c, out_x


def rotary_2d(t):
    n = t.shape[1]
    pos = jnp.arange(n)
    half = RET_HEAD // 2
    inv = ROPE_BASE ** (-jnp.arange(0, half, 2, dtype=jnp.float32) / half)

    def rot(xp, p):
        ang = p.astype(jnp.float32)[:, None] * inv
        cos = jnp.cos(ang)[None, :, None, :]
        sin = jnp.sin(ang)[None, :, None, :]
        x1, x2 = xp[..., :half // 2], xp[..., half // 2:]
        return jnp.concatenate([x1 * cos - x2 * sin, x1 * sin + x2 * cos], axis=-1)

    return jnp.concatenate([rot(t[..., :half], pos // GRID_W), rot(t[..., half:], pos % GRID_W)], axis=-1)


def retention_scan(q, k, v, log_gamma, s0, include_diag):
    b, n, h, _ = q.shape
    nc = n // RET_CHUNK

    def chunks(t):
        return t.reshape(b, nc, RET_CHUNK, h, t.shape[-1]).transpose(1, 0, 3, 2, 4)

    idx = jnp.arange(RET_CHUNK, dtype=jnp.float32)
    diff = idx[:, None] - idx[None, :]
    keep = diff >= 0 if include_diag else diff > 0
    dmat = jnp.where(keep[None], jnp.exp(jnp.maximum(diff, 0.0)[None] * log_gamma[:, None, None]), 0.0)
    xi = jnp.exp((idx + 1.0)[None, :] * log_gamma[:, None])[..., None]
    zeta = jnp.exp((RET_CHUNK - 1.0 - idx)[None, :] * log_gamma[:, None])[..., None]
    g_chunk = jnp.exp(RET_CHUNK * log_gamma)[:, None, None]

    def step(s, inp):
        qc, kc, vc = inp
        scores = jnp.einsum('bhnd,bhmd->bhnm', qc, kc) * dmat
        y = jnp.einsum('bhnm,bhmv->bhnv', scores, vc) + jnp.einsum('bhnd,bhdv->bhnv', qc * xi, s)
        s = s * g_chunk + jnp.einsum('bhmd,bhmv->bhdv', kc * zeta, vc)
        return s, y

    s_fin, ys = lax.scan(step, s0, (chunks(q), chunks(k), chunks(v)))
    return ys.transpose(1, 0, 3, 2, 4).reshape(b, n, h, v.shape[-1]), s_fin


def retention_inputs(p, rotate):
    b, n, _ = p.shape
    q, k, v, g = jnp.split(p.astype(jnp.float32), 4, axis=-1)

    def heads(t):
        return t.reshape(b, n, RET_HEADS, RET_HEAD)

    q, k, v = heads(q), heads(k) * (RET_HEAD ** -0.5), heads(v)
    if rotate:
        q, k = rotary_2d(q), rotary_2d(k)
    return q, k, v, g


def retention_output(y, g, gn_w):
    b, n = y.shape[:2]
    yn = y * lax.rsqrt(jnp.mean(y * y, axis=-1, keepdims=True) + NORM_EPS)
    return yn.reshape(b, n, RET_WIDTH) * gn_w * jax.nn.silu(g)


def retention_mixer(p_c, p_x, log2_decay, gn_w, with_ctx):
    qc, kc, vc, gc = retention_inputs(p_c, False)
    qx, kx, vx, gx = retention_inputs(p_x, True)
    log_gamma = jnp.log1p(-jnp.exp2(-log2_decay.astype(jnp.float32)))
    s0 = jnp.zeros((p_x.shape[0], RET_HEADS, RET_HEAD, RET_HEAD), jnp.float32)

    def flip(t):
        return jnp.flip(t, axis=1)

    yc_f, sc_f = retention_scan(qc, kc, vc, log_gamma[0], s0, True)
    yx_f, _ = retention_scan(qx, kx, vx, log_gamma[0], sc_f, True)
    yc_b, sc_b = retention_scan(flip(qc), flip(kc), flip(vc), log_gamma[1], s0, False)
    yx_b, _ = retention_scan(flip(qx), flip(kx), flip(vx), log_gamma[1], sc_b, False)
    out_x = retention_output(yx_f + flip(yx_b), gx, gn_w)
    out_c = retention_output(yc_f + flip(yc_b), gc, gn_w) if with_ctx else None
    return out_c, out_x


def route(h, router_w, router_b):
    t = h.shape[0]
    scores = jax.nn.sigmoid((h @ router_w).astype(jnp.float32))
    biased = scores + router_b.astype(jnp.float32)
    grp = biased.reshape(t, N_GROUPS, EXPERTS_PER_GROUP)
    grp_score = jnp.sum(lax.top_k(grp, 2)[0], axis=-1)
    g_sel = jnp.argmax(grp_score, axis=-1)
    sel_idx = jnp.broadcast_to(g_sel[:, None, None], (t, 1, EXPERTS_PER_GROUP))
    in_grp = jnp.take_along_axis(grp, sel_idx, axis=1)[:, 0]
    _, local = lax.top_k(in_grp, TOP_K)
    expert = g_sel[:, None] * EXPERTS_PER_GROUP + local
    w = jnp.take_along_axis(scores, expert, axis=1)
    return expert, w / jnp.sum(w, axis=-1, keepdims=True)


def moe(h, router_w, router_b, w1, w2, layer):
    t, d = h.shape
    expert, gate = route(h, router_w, router_b)
    m = t * TOP_K
    e_flat = expert.reshape(-1)
    tok = jnp.repeat(jnp.arange(t, dtype=jnp.int32), TOP_K)
    order = jnp.argsort(e_flat)
    e_s = e_flat[order]
    counts = jnp.bincount(e_flat, length=N_EXPERTS)
    padded = (counts + MOE_BLOCK - 1) // MOE_BLOCK * MOE_BLOCK
    start = jnp.cumsum(counts) - counts
    pend = jnp.cumsum(padded)
    pstart = pend - padded
    dest = pstart[e_s] + jnp.arange(m, dtype=jnp.int32) - start[e_s]
    m_pad = -(-m // MOE_BLOCK) * MOE_BLOCK + N_EXPERTS * MOE_BLOCK
    n_blocks = m_pad // MOE_BLOCK
    row_tok = jnp.zeros((m_pad,), jnp.int32).at[dest].set(tok[order])
    row_w = jnp.zeros((m_pad,), h.dtype).at[dest].set(gate.reshape(-1)[order].astype(h.dtype))
    block_start = jnp.arange(n_blocks, dtype=jnp.int32) * MOE_BLOCK
    block_expert = jnp.minimum(jnp.searchsorted(pend, block_start, side='right'), N_EXPERTS - 1)
    xb = h[row_tok].reshape(n_blocks, MOE_BLOCK, d)

    def expert_block(args):
        xblk, e = args
        gu = xblk @ w1[layer, e]
        g_, u_ = jnp.split(gu, 2, axis=-1)
        return (jax.nn.silu(g_) * u_) @ w2[layer, e]

    yb = lax.map(expert_block, (xb, block_expert)).reshape(m_pad, d)
    return jnp.zeros_like(h).at[row_tok].add(yb * row_w[:, None])


def setup_inputs(seed: int = 0) -> dict:
    key = jax.random.key(seed)
    ks = jax.random.split(key, 28)
    f32 = jnp.float32

    def nrm(k, shape, s):
        return s * jax.random.normal(k, shape, f32)

    D = D_MODEL
    return {
        "x": nrm(ks[0], (BATCH, SEQ, D), 1.0),
        "c": nrm(ks[1], (BATCH, D), 1.0),
        "ctx": nrm(ks[2], (BATCH, CTX_LEN, D), 1.0),
        "c_ctx": nrm(ks[3], (D,), 1.0),
        "ada_w": nrm(ks[4], (DEPTH, D, 6 * D), 0.3 * D ** -0.5),
        "ada_b": nrm(ks[5], (DEPTH, 6 * D), 0.02),
        "norm1_g": 1.0 + nrm(ks[6], (DEPTH, D), 0.05),
        "norm2_g": 1.0 + nrm(ks[7], (DEPTH, D), 0.05),
        "w_in": nrm(ks[8], (DEPTH, D, P_IN), D ** -0.5),
        "shift_mu": jax.random.uniform(ks[9], (DEPTH, 4, RWKV_IN), f32, 0.0, 0.5),
        "rwkv_w0": jax.random.uniform(ks[10], (DEPTH, 2, RWKV_WIDTH), f32, -6.0, -1.0),
        "rwkv_w2": nrm(ks[11], (DEPTH, 2, DECAY_LORA, RWKV_WIDTH), 0.5 * DECAY_LORA ** -0.5),
        "rwkv_a0": nrm(ks[12], (DEPTH, 2, RWKV_WIDTH), 0.5),
        "rwkv_a2": nrm(ks[13], (DEPTH, 2, ICLR_LORA, RWKV_WIDTH), 0.5 * ICLR_LORA ** -0.5),
        "rwkv_g2": nrm(ks[14], (DEPTH, GATE_LORA, RWKV_WIDTH), GATE_LORA ** -0.5),
        "rwkv_k_k": 0.85 + nrm(ks[15], (DEPTH, RWKV_WIDTH), 0.05),
        "rwkv_k_a": 1.0 + nrm(ks[16], (DEPTH, RWKV_WIDTH), 0.05),
        "rwkv_r_k": nrm(ks[17], (DEPTH, RWKV_HEADS, RWKV_HEAD), 0.1),
        "rwkv_lnx_w": 1.0 + nrm(ks[18], (DEPTH, RWKV_WIDTH), 0.05),
        "rwkv_lnx_b": nrm(ks[19], (DEPTH, RWKV_WIDTH), 0.02),
        "ret_log2_decay": ((5.0 + jnp.arange(RET_HEADS, dtype=f32))[None, None, :]
                           + jnp.array([0.0, 0.5], f32)[None, :, None]
                           + nrm(ks[20], (DEPTH, 2, RET_HEADS), 0.1)),
        "ret_gn_w": 1.0 + nrm(ks[21], (DEPTH, RET_WIDTH), 0.05),
        "w_out": nrm(ks[22], (DEPTH, D, D), D ** -0.5),
        "router_w": nrm(ks[23], (D, N_EXPERTS), D ** -0.5),
        "router_b": nrm(ks[24], (N_EXPERTS,), 0.01),
        "moe_w1": nrm(ks[25], (DEPTH, N_EXPERTS, D, 2 * D_EXPERT), D ** -0.5),
        "moe_w2": nrm(ks[26], (DEPTH, N_EXPERTS, D_EXPERT, D), D_EXPERT ** -0.5),
        "final_g": 1.0 + nrm(ks[27], (D,), 0.05),
    }


def reference(x, c, ctx, c_ctx, ada_w, ada_b, norm1_g, norm2_g, w_in, shift_mu,
              rwkv_w0, rwkv_w2, rwkv_a0, rwkv_a2, rwkv_g2, rwkv_k_k, rwkv_k_a, rwkv_r_k,
              rwkv_lnx_w, rwkv_lnx_b, ret_log2_decay, ret_gn_w, w_out, router_w, router_b,
              moe_w1, moe_w2, final_g):
    b, n, d = x.shape
    rows = n // GRID_W
    n_ctx = ctx.shape[1]
    cx = ctx
    for l in range(DEPTH):
        last = l == DEPTH - 1
        mod_x = jax.nn.silu(c) @ ada_w[l] + ada_b[l]
        mod_c = jax.nn.silu(c_ctx) @ ada_w[l] + ada_b[l]
        sh1, sc1, gt1, sh2, sc2, gt2 = jnp.split(mod_x[:, None, :], 6, axis=-1)
        csh1, csc1, cgt1, csh2, csc2, cgt2 = jnp.split(mod_c, 6, axis=-1)

        h_x = rms_norm(x, norm1_g[l]) * (1.0 + sc1) + sh1
        h_c = rms_norm(cx, norm1_g[l]) * (1.0 + csc1) + csh1
        p = jnp.concatenate([h_c, h_x], axis=1) @ w_in[l]
        p_c, p_x = p[:, :n_ctx], p[:, n_ctx:]
        u_c = shift_seq(p_c[..., :RWKV_IN], shift_mu[l, 0], shift_mu[l, 1])
        u_x = shift_grid(p_x[..., :RWKV_IN], shift_mu[l], rows)
        ya_c, ya_x = rwkv7_mixer(u_c, u_x, rwkv_w0[l], rwkv_w2[l], rwkv_a0[l], rwkv_a2[l], rwkv_g2[l],
                                 rwkv_k_k[l], rwkv_k_a[l], rwkv_r_k[l], rwkv_lnx_w[l], rwkv_lnx_b[l],
                                 not last)
        yb_c, yb_x = retention_mixer(p_c[..., RWKV_IN:], p_x[..., RWKV_IN:], ret_log2_decay[l],
                                     ret_gn_w[l], not last)

        if last:
            y = jnp.concatenate([ya_x, yb_x], axis=-1).astype(x.dtype) @ w_out[l]
            x = x + gt1 * y
            h2 = rms_norm(x, norm2_g[l]) * (1.0 + sc2) + sh2
            f = moe(h2.reshape(-1, d), router_w, router_b, moe_w1, moe_w2, l).reshape(b, n, d)
            x = x + gt2 * f
        else:
            y_all = jnp.concatenate([jnp.concatenate([ya_c, yb_c], axis=-1),
                                     jnp.concatenate([ya_x, yb_x], axis=-1)], axis=1)
            y = y_all.astype(x.dtype) @ w_out[l]
            cx = cx + cgt1 * y[:, :n_ctx]
            x = x + gt1 * y[:, n_ctx:]
            h2 = jnp.concatenate([rms_norm(cx, norm2_g[l]) * (1.0 + csc2) + csh2,
                                  rms_norm(x, norm2_g[l]) * (1.0 + sc2) + sh2], axis=1)
            f = moe(h2.reshape(-1, d), router_w, router_b, moe_w1, moe_w2, l).reshape(b, n_ctx + n, d)
            cx = cx + cgt2 * f[:, :n_ctx]
            x = x + gt2 * f[:, n_ctx:]
    return rms_norm(x, final_g)
```

```python
import functools
import math

import jax
import jax.numpy as jnp
from jax import lax
from jax.experimental import pallas as pl
from jax.experimental.pallas import tpu as pltpu

F32 = jnp.float32
BF16 = jnp.bfloat16
HIGHEST = lax.Precision.HIGHEST

D_MODEL = 2048
CTX_LEN = 256
GRID_W = 64
NORM_EPS = 1e-6

RWKV_WIDTH = 1024
RWKV_HEAD = 64
DECAY_LORA = 64
ICLR_LORA = 64
GATE_LORA = 160
LNX_EPS = 64e-5
RWKV_IN = 3 * RWKV_WIDTH + 2 * DECAY_LORA + 2 * ICLR_LORA + GATE_LORA
LORA_PAD = 512
RWKV_PAD = 3 * RWKV_WIDTH + LORA_PAD

RET_WIDTH = 1024
RET_HEAD = 128
RET_HEADS = 8
RET_CHUNK = 128
ROPE_BASE = 10000.0
P_PAD = RWKV_PAD + 4 * RET_WIDTH

N_EXPERTS = 32
N_GROUPS = 4
EXPERTS_PER_GROUP = 8
TOP_K = 2
D_EXPERT = 1024
MOE_BLOCK = 128

LANE = 128
TOKEN_TILE = 256
RWKV_CHUNK = 64
VMEM_LIMIT = 48 * 1024 * 1024


def _cparams(sem):
    return pltpu.CompilerParams(dimension_semantics=sem, vmem_limit_bytes=VMEM_LIMIT)


def _dot(a, b, precision=None):
    return jnp.dot(a, b, preferred_element_type=F32, precision=precision)


def _dot_nt(a, b, precision=None):
    return lax.dot_general(a, b, (((1,), (1,)), ((), ())), preferred_element_type=F32, precision=precision)


def _dot_tn(a, b, precision=None):
    return lax.dot_general(a, b, (((0,), (0,)), ((), ())), preferred_element_type=F32, precision=precision)


def _split_bf16(x):
    hi = x.astype(BF16)
    lo = (x - hi.astype(F32)).astype(BF16)
    return hi, lo


def _head_sum(x, head):
    ri = lax.broadcasted_iota(jnp.int32, (LANE, LANE), 0) // head
    ci = lax.broadcasted_iota(jnp.int32, (LANE, LANE), 1) // head
    bd = (ri == ci).astype(BF16)
    outs = []
    for j in range(x.shape[1] // LANE):
        hi, lo = _split_bf16(x[:, j * LANE:(j + 1) * LANE])
        outs.append(_dot(hi, bd) + _dot(lo, bd))
    return outs[0] if len(outs) == 1 else jnp.concatenate(outs, axis=1)


def _ada_kernel(cct_ref, w_ref, b_ref, o_ref):
    k = pl.program_id(2)

    @pl.when(k == 0)
    def _():
        o_ref[0] = jnp.broadcast_to(b_ref[0], o_ref.shape[1:])

    s = cct_ref[...]
    s = s * jax.nn.sigmoid(s)
    w = w_ref[0]
    acc0 = jnp.sum(s[:, 0:1] * w, axis=0, keepdims=True)
    acc1 = jnp.sum(s[:, 1:2] * w, axis=0, keepdims=True)
    o_ref[0] += jnp.concatenate([acc0, acc1], axis=0)


def _ada_mod(cct, ada_w, ada_b):
    depth, d, n6 = ada_w.shape
    tk, tn = 512, 2048
    return pl.pallas_call(
        _ada_kernel,
        grid=(depth, n6 // tn, d // tk),
        in_specs=[
            pl.BlockSpec((tk, 2), lambda l, j, k: (k, 0)),
            pl.BlockSpec((1, tk, tn), lambda l, j, k: (l, k, j)),
            pl.BlockSpec((1, 1, tn), lambda l, j, k: (l, 0, j)),
        ],
        out_specs=pl.BlockSpec((1, 2, tn), lambda l, j, k: (l, 0, j)),
        out_shape=jax.ShapeDtypeStruct((depth, 2, n6), F32),
        compiler_params=_cparams(("parallel", "parallel", "arbitrary")),
        name="ada_mod",
    )(cct, ada_w, ada_b.reshape(depth, 1, n6))


def _rms(x):
    return x * lax.rsqrt(jnp.mean(x * x, axis=-1, keepdims=True) + NORM_EPS)


def _norm_mod_kernel(x_ref, g_ref, sc_ref, sh_ref, o_ref):
    is_ctx = pl.program_id(0) == 0
    y = _rms(x_ref[...]) * g_ref[...]
    sc = jnp.where(is_ctx, sc_ref[1:2, :], sc_ref[0:1, :])
    sh = jnp.where(is_ctx, sh_ref[1:2, :], sh_ref[0:1, :])
    o_ref[...] = (y * (1.0 + sc) + sh).astype(o_ref.dtype)


def _norm_mod(xa, g, sc, sh):
    t, d = xa.shape
    row = pl.BlockSpec((1, d), lambda i: (0, 0))
    two = pl.BlockSpec((2, d), lambda i: (0, 0))
    return pl.pallas_call(
        _norm_mod_kernel,
        grid=(t // TOKEN_TILE,),
        in_specs=[pl.BlockSpec((TOKEN_TILE, d), lambda i: (i, 0)), row, two, two],
        out_specs=pl.BlockSpec((TOKEN_TILE, d), lambda i: (i, 0)),
        out_shape=jax.ShapeDtypeStruct((t, d), BF16),
        compiler_params=_cparams(("parallel",)),
        name="norm_mod",
    )(xa, g.reshape(1, d), sc, sh)


def _matmul_kernel(a_ref, b_ref, o_ref):
    o_ref[...] = _dot(a_ref[...], b_ref[...]).astype(o_ref.dtype)


def _row_tile(t, cap):
    tm = TOKEN_TILE
    for cand in range(TOKEN_TILE, cap + 1, TOKEN_TILE):
        if t % cand == 0:
            tm = cand
    return tm


def _matmul(a, b, out_dtype):
    m, k = a.shape
    n = b.shape[1]
    tm, tn = _row_tile(m, 1280), 512
    return pl.pallas_call(
        _matmul_kernel,
        grid=(m // tm, n // tn),
        in_specs=[pl.BlockSpec((tm, k), lambda i, j: (i, 0)), pl.BlockSpec((k, tn), lambda i, j: (0, j))],
        out_specs=pl.BlockSpec((tm, tn), lambda i, j: (i, j)),
        out_shape=jax.ShapeDtypeStruct((m, n), out_dtype),
        compiler_params=_cparams(("parallel", "arbitrary")),
        name="in_proj",
    )(a, b)


def _rwkv_prep_kernel(pm_ref, pp_ref, pn_ref, mu_ref, w2_ref, a2_ref, g2_ref, w0_ref, a0_ref, kk_ref, ka_ref,
                      rk_ref, r_out, v_out, kk_out, g_out, bonus_out, lw_out, kd_out, b_out):
    i = pl.program_id(0)
    nt = pl.num_programs(0)
    is_ctx = i == 0
    tm = TOKEN_TILE
    row = lax.broadcasted_iota(jnp.int32, (tm, 1), 0)
    in_row = row & (GRID_W - 1)
    edge_lo = jnp.where(is_ctx, row, in_row) == 0
    edge_hi = jnp.where(is_ctx, row - (tm - 1), in_row - (GRID_W - 1)) == 0
    no_up = jnp.logical_or(is_ctx, i == 1)
    no_down = jnp.logical_or(is_ctx, i == nt - 1)
    x_gate = jnp.where(is_ctx, 0.0, 1.0)

    def shifted(c0, c1):
        p = pm_ref[:, c0:c1].astype(F32)
        mu = mu_ref[:, c0:c1]
        prev = jnp.where(edge_lo, 0.0, pltpu.roll(p, 1, axis=0))
        nxt = jnp.where(edge_hi, 0.0, pltpu.roll(p, tm - 1, axis=0))
        out = p + mu[0:1] * (prev - p) + mu[1:2] * (nxt - p)
        above = jnp.where(no_up, 0.0, pp_ref[:, c0:c1].astype(F32))
        below = jnp.where(no_down, 0.0, pn_ref[:, c0:c1].astype(F32))
        up = jnp.concatenate([above, p[:tm - GRID_W]], axis=0)
        down = jnp.concatenate([p[GRID_W:], below], axis=0)
        return out + x_gate * (mu[2:3] * (up - p) + mu[3:4] * (down - p))

    w = RWKV_WIDTH
    ul = shifted(3 * w, 3 * w + LORA_PAD)
    twd = jnp.tanh(ul[:, 0:LANE]).astype(BF16)
    ad = ul[:, LANE:2 * LANE].astype(BF16)
    sg = jax.nn.sigmoid(ul[:, 2 * LANE:4 * LANE]).astype(BF16)
    g_out[...] = _dot(sg, g2_ref[...]).astype(g_out.dtype)

    r = shifted(0, w)
    k = shifted(w, 2 * w)
    v = shifted(2 * w, 3 * w)
    r_out[...] = r.astype(r_out.dtype)
    v_out[...] = v.astype(v_out.dtype)
    kkr = k * kk_ref[...]
    nrm = jnp.sqrt(_head_sum(kkr * kkr, RWKV_HEAD))
    kk = kkr / jnp.maximum(nrm, 1e-12)
    kk_out[...] = kk.astype(kk_out.dtype)
    ksum = jnp.zeros_like(k)
    for d in range(2):
        z = w0_ref[d:d + 1, :] + _dot(twd, w2_ref[d])
        lw_out[d] = -jax.nn.sigmoid(z) * math.exp(-0.5)
        a = jax.nn.sigmoid(a0_ref[d:d + 1, :] + _dot(ad, a2_ref[d]))
        kd = k * (1.0 + (a - 1.0) * ka_ref[...])
        kd_out[d] = kd.astype(kd_out.dtype)
        b_out[d] = (a * kk).astype(b_out.dtype)
        ksum = ksum + kd
    bonus = _head_sum(r * ksum * rk_ref[...], RWKV_HEAD) * v
    bonus_out[...] = bonus.astype(bonus_out.dtype)


def _rwkv_prep(p, mu, w2p, a2p, g2p, w0, a0, k_k, k_a, r_k):
    t = p.shape[0]
    nt = t // TOKEN_TILE
    per_tile = TOKEN_TILE // GRID_W
    n_rows = t // GRID_W
    w = RWKV_WIDTH
    full = lambda shape: pl.BlockSpec(shape, lambda i: (0,) * len(shape))
    tok = pl.BlockSpec((TOKEN_TILE, w), lambda i: (i, 0))
    tok2 = pl.BlockSpec((2, TOKEN_TILE, w), lambda i: (0, i, 0))
    one = jax.ShapeDtypeStruct((t, w), BF16)
    return pl.pallas_call(
        _rwkv_prep_kernel,
        grid=(nt,),
        in_specs=[
            pl.BlockSpec((TOKEN_TILE, RWKV_PAD), lambda i: (i, 0)),
            pl.BlockSpec((GRID_W, RWKV_PAD), lambda i: (jnp.maximum(i * per_tile - 1, 0), 0)),
            pl.BlockSpec((GRID_W, RWKV_PAD), lambda i: (jnp.minimum((i + 1) * per_tile, n_rows - 1), 0)),
            full((4, RWKV_PAD)), full((2, LANE, w)), full((2, LANE, w)), full((2 * LANE, w)),
            full((2, w)), full((2, w)), full((1, w)), full((1, w)), full((1, w)),
        ],
        out_specs=[tok, tok, tok, tok, tok, tok2, tok2, tok2],
        out_shape=[one, one, one, one, one,
                   jax.ShapeDtypeStruct((2, t, w), F32),
                   jax.ShapeDtypeStruct((2, t, w), BF16),
                   jax.ShapeDtypeStruct((2, t, w), BF16)],
        compiler_params=_cparams(("parallel",)),
        name="rwkv_prep",
    )(p, p, p, mu, w2p, a2p, g2p, w0, a0, k_k, k_a, r_k)


def _rwkv_unit(r, v, kk, lw, kd, b, m_state, sign):
    c = RWKV_CHUNK
    row = lax.broadcasted_iota(jnp.int32, (c, c), 0)
    col = lax.broadcasted_iota(jnp.int32, (c, c), 1)
    rel = (col - row) * sign
    incl = rel <= 0
    strict = rel < 0
    lane = lax.broadcasted_iota(jnp.int32, (1, LANE), 1)
    head_mask = (lane < RWKV_HEAD, lane >= RWKV_HEAD)

    lc = _dot(incl.astype(F32), lw, precision=HIGHEST)
    lend = jnp.sum(lw, axis=0, keepdims=True)
    e_pos = jnp.exp(lc)
    e_neg = jnp.exp(-lc)
    e_end = jnp.exp(lend - lc)
    rl = r * e_pos
    kke = kk * jnp.exp(lc - lw)
    kinv = (kd * e_neg).astype(BF16)
    binv = (b * e_neg).astype(BF16)
    kdec = (kd * e_end).astype(BF16)
    bdec = (b * e_end).astype(BF16)
    vb = v.astype(BF16)
    eye = (row == col).astype(F32)

    w1 = jnp.zeros((c, LANE), F32)
    w2 = jnp.zeros((c, LANE), F32)
    qeff = rl
    yloc = jnp.zeros((c, LANE), F32)
    for h in range(2):
        kkeh = jnp.where(head_mask[h], kke, 0.0).astype(BF16)
        rlh = jnp.where(head_mask[h], rl, 0.0).astype(BF16)
        a_kk = jnp.where(strict, _dot_nt(kkeh, kinv), 0.0).astype(BF16)
        a_kb = jnp.where(strict, _dot_nt(kkeh, binv), 0.0)
        a_rk = jnp.where(incl, _dot_nt(rlh, kinv), 0.0).astype(BF16)
        a_rb = jnp.where(incl, _dot_nt(rlh, binv), 0.0).astype(BF16)
        x = -a_kb
        tinv = eye + x
        pw = x
        for _ in range(5):
            pwb = pw.astype(BF16)
            pw = _dot(pwb, pwb)
            tinv = tinv + _dot(tinv.astype(BF16), pw.astype(BF16))
        tb = tinv.astype(BF16)
        w1h = _dot(tb, kkeh)
        w2h = jnp.where(head_mask[h], _dot(tb, _dot(a_kk, vb).astype(BF16)), 0.0)
        w1 = w1 + w1h
        w2 = w2 + w2h
        qeff = qeff - _dot(a_rb, w1h.astype(BF16))
        yloc = yloc + jnp.where(head_mask[h], _dot(a_rk, vb) - _dot(a_rb, w2h.astype(BF16)), 0.0)

    ri = lax.broadcasted_iota(jnp.int32, (LANE, LANE), 0)
    ci = lax.broadcasted_iota(jnp.int32, (LANE, LANE), 1)
    same_head = (ri < RWKV_HEAD) == (ci < RWKV_HEAD)
    pm = jnp.where(same_head, -_dot_tn(bdec, w1.astype(BF16)), 0.0)
    mloc = jnp.where(same_head, _dot_tn(kdec, vb) - _dot_tn(bdec, w2.astype(BF16)), 0.0)
    diag = jnp.where(ri == ci, jnp.broadcast_to(lend, (LANE, LANE)), 0.0)
    lcol = _dot(diag, jnp.ones((LANE, LANE), F32), precision=HIGHEST)
    y = _dot(qeff, m_state, precision=HIGHEST) + yloc
    m_new = jnp.exp(lcol) * m_state + _dot(pm, m_state, precision=HIGHEST) + mloc
    return y, m_new


def _rwkv_scan_kernel(r_ref, v_ref, kk_ref, lw_ref, kd_ref, b_ref, y_ref, m_ref):
    d = pl.program_id(0)
    s = pl.program_id(2)
    sign = 1 - 2 * d
    n_chunks = TOKEN_TILE // RWKV_CHUNK
    n_pairs = r_ref.shape[1] // LANE

    @pl.when(s == 0)
    def _():
        m_ref[...] = jnp.zeros_like(m_ref)

    def chunk_body(cc, carry):
        c = jnp.where(d == 1, n_chunks - 1 - cc, cc)
        rows = pl.ds(pl.multiple_of(c * RWKV_CHUNK, RWKV_CHUNK), RWKV_CHUNK)
        for pr in range(n_pairs):
            lanes = slice(pr * LANE, (pr + 1) * LANE)
            y, m_new = _rwkv_unit(
                r_ref[rows, lanes].astype(F32), v_ref[rows, lanes].astype(F32), kk_ref[rows, lanes].astype(F32),
                lw_ref[rows, lanes], kd_ref[rows, lanes].astype(F32), b_ref[rows, lanes].astype(F32),
                m_ref[pr], sign)
            y_ref[rows, lanes] = y.astype(y_ref.dtype)
            m_ref[pr] = m_new
        return carry

    lax.fori_loop(0, n_chunks, chunk_body, 0)


def _rwkv_scan(r, v, kk, lw, kd, b):
    t, w = r.shape
    nt = t // TOKEN_TILE
    wb = 2 * LANE

    def tile(d, s):
        return jnp.where(jnp.logical_and(d == 1, s > 0), nt - s, s)

    shared = pl.BlockSpec((TOKEN_TILE, wb), lambda d, j, s: (tile(d, s), j))
    per_dir = pl.BlockSpec((None, TOKEN_TILE, wb), lambda d, j, s: (d, tile(d, s), j))
    return pl.pallas_call(
        _rwkv_scan_kernel,
        grid=(2, w // wb, nt),
        in_specs=[shared, shared, shared, per_dir, per_dir, per_dir],
        out_specs=per_dir,
        out_shape=jax.ShapeDtypeStruct((2, t, w), BF16),
        scratch_shapes=[pltpu.VMEM((wb // LANE, LANE, LANE), F32)],
        compiler_params=_cparams(("parallel", "parallel", "arbitrary")),
        name="rwkv_scan",
    )(r, v, kk, lw, kd, b)


def _rotate(t, cos, sin_signed):
    lane = lax.broadcasted_iota(jnp.int32, (1, LANE), 1)
    first = (lane & 63) < 32
    swapped = jnp.where(first, pltpu.roll(t, LANE - 32, axis=1), pltpu.roll(t, 32, axis=1))
    return t * cos + swapped * sin_signed


def _ret_qk(q_ref, k_ref, cos_ref, sin_ref):
    cos = cos_ref[...]
    sin = sin_ref[...]
    q = _rotate(q_ref[...].astype(F32), cos, sin)
    k = _rotate(k_ref[...].astype(F32) * (RET_HEAD ** -0.5), cos, sin)
    return q, k


def _log_gamma(l2d_ref):
    return jnp.log1p(-jnp.exp2(-l2d_ref[0]))


def _ret_fwd_kernel(q_ref, k_ref, v_ref, cos_ref, sin_ref, l2d_ref, y_ref, s_ref):
    @pl.when(pl.program_id(1) == 0)
    def _():
        s_ref[...] = jnp.zeros_like(s_ref)

    c = RET_CHUNK
    q, k = _ret_qk(q_ref, k_ref, cos_ref, sin_ref)
    lg = jnp.broadcast_to(_log_gamma(l2d_ref), (c, LANE))
    idx = lax.broadcasted_iota(jnp.int32, (c, LANE), 0).astype(F32)
    xi = jnp.exp((idx + 1.0) * lg)
    zeta = jnp.exp((c - 1.0 - idx) * lg)
    s = s_ref[...]
    y_ref[...] = _dot((q * xi).astype(BF16), s.astype(BF16)).astype(y_ref.dtype)
    s_ref[...] = s * jnp.exp(c * lg) + _dot_tn((k * zeta).astype(BF16), v_ref[...])


def _ret_bwd_kernel(q_ref, k_ref, v_ref, g_ref, y1_ref, cos_ref, sin_ref, l2f_ref, l2b_ref, gn_ref, o_ref, s_ref):
    @pl.when(pl.program_id(1) == 0)
    def _():
        s_ref[...] = jnp.zeros_like(s_ref)

    c = RET_CHUNK
    q, k = _ret_qk(q_ref, k_ref, cos_ref, sin_ref)
    v = v_ref[...]
    lgf = jnp.broadcast_to(_log_gamma(l2f_ref), (c, LANE))
    lgb = jnp.broadcast_to(_log_gamma(l2b_ref), (c, LANE))
    rowi = lax.broadcasted_iota(jnp.int32, (c, LANE), 0)
    coli = lax.broadcasted_iota(jnp.int32, (c, LANE), 1)
    diff = (rowi - coli).astype(F32)
    idx = rowi.astype(F32)
    dmat = jnp.where(rowi >= coli, jnp.exp(diff * lgf), jnp.exp(-diff * lgb))
    xi = jnp.exp((c - idx) * lgb)
    zeta = jnp.exp(idx * lgb)
    qb = q.astype(BF16)
    kb = k.astype(BF16)
    scores = (_dot_nt(qb, kb) * dmat).astype(BF16)
    s = s_ref[...]
    y = y1_ref[...].astype(F32) + _dot(scores, v) + _dot((q * xi).astype(BF16), s.astype(BF16))
    s_ref[...] = s * jnp.exp(c * lgb) + _dot_tn((k * zeta).astype(BF16), v)
    yn = y * lax.rsqrt(jnp.mean(y * y, axis=-1, keepdims=True) + NORM_EPS)
    g = g_ref[...].astype(F32)
    o_ref[...] = (yn * gn_ref[...] * (g * jax.nn.sigmoid(g))).astype(o_ref.dtype)


def _retention(p, cos, sin, l2d, gn_w):
    t = p.shape[0]
    nc = t // RET_CHUNK
    n_ctx = CTX_LEN // RET_CHUNK
    base = RWKV_PAD // LANE
    nh = RET_HEADS

    def col(part):
        return lambda h, s: (s, base + part * nh + h)

    def rcol(part):
        return lambda h, s: (rchunk(s), base + part * nh + h)

    def rchunk(s):
        return jnp.where(s < n_ctx, n_ctx - 1 - s, nc + n_ctx - 1 - s)

    blk = (RET_CHUNK, LANE)
    dec = lambda d: pl.BlockSpec((None, None, 1, LANE), lambda h, s: (d, h, 0, 0))
    y1 = pl.pallas_call(
        _ret_fwd_kernel,
        grid=(nh, nc),
        in_specs=[pl.BlockSpec(blk, col(0)), pl.BlockSpec(blk, col(1)), pl.BlockSpec(blk, col(2)),
                  pl.BlockSpec(blk, lambda h, s: (s, 0)), pl.BlockSpec(blk, lambda h, s: (s, 0)), dec(0)],
        out_specs=pl.BlockSpec(blk, lambda h, s: (s, h)),
        out_shape=jax.ShapeDtypeStruct((t, RET_WIDTH), BF16),
        scratch_shapes=[pltpu.VMEM((RET_HEAD, RET_HEAD), F32)],
        compiler_params=_cparams(("parallel", "arbitrary")),
        name="ret_fwd",
    )(p, p, p, cos, sin, l2d)
    return pl.pallas_call(
        _ret_bwd_kernel,
        grid=(nh, nc),
        in_specs=[pl.BlockSpec(blk, rcol(0)), pl.BlockSpec(blk, rcol(1)), pl.BlockSpec(blk, rcol(2)),
                  pl.BlockSpec(blk, rcol(3)), pl.BlockSpec(blk, lambda h, s: (rchunk(s), h)),
                  pl.BlockSpec(blk, lambda h, s: (rchunk(s), 0)), pl.BlockSpec(blk, lambda h, s: (rchunk(s), 0)),
                  dec(0), dec(1), pl.BlockSpec((1, LANE), lambda h, s: (0, h))],
        out_specs=pl.BlockSpec(blk, lambda h, s: (rchunk(s), h)),
        out_shape=jax.ShapeDtypeStruct((t, RET_WIDTH), BF16),
        scratch_shapes=[pltpu.VMEM((RET_HEAD, RET_HEAD), F32)],
        compiler_params=_cparams(("parallel", "arbitrary")),
        name="ret_bwd",
    )(p, p, p, p, y1, cos, sin, l2d, l2d, gn_w.reshape(1, RET_WIDTH))


def _mix_out_kernel(ya_ref, bonus_ref, g_ref, yb_ref, x_ref, wo_ref, lnw_ref, lnb_ref, gt_ref, n2_ref, sc_ref,
                    sh_ref, rw_ref, x_out, h_out, s_out):
    is_ctx = pl.program_id(0) == 0
    y = ya_ref[0].astype(F32) + ya_ref[1].astype(F32)
    inv_n = 1.0 / RWKV_HEAD
    mu = _head_sum(y, RWKV_HEAD) * inv_n
    yc = y - mu
    var = _head_sum(yc * yc, RWKV_HEAD) * inv_n
    yn = yc * lax.rsqrt(var + LNX_EPS)
    ya = (yn * lnw_ref[...] + lnb_ref[...] + bonus_ref[...].astype(F32)) * g_ref[...].astype(F32)
    z = _dot(ya.astype(BF16), wo_ref[0:RWKV_WIDTH, :]) + _dot(yb_ref[...], wo_ref[RWKV_WIDTH:, :])
    pick = lambda ref: jnp.where(is_ctx, ref[1:2, :], ref[0:1, :])
    xn = x_ref[...] + pick(gt_ref) * z
    x_out[...] = xn
    h = _rms(xn) * n2_ref[...] * (1.0 + pick(sc_ref)) + pick(sh_ref)
    h_out[...] = h
    s_out[...] = jax.nn.sigmoid(_dot_nt(rw_ref[...], h, precision=HIGHEST))


def _mix_out(ya, bonus, g, yb, xa, wo, lnw, lnb, gt, n2, sc, sh, rwt):
    t, d = xa.shape
    w = RWKV_WIDTH
    tm = TOKEN_TILE
    row = lambda n: pl.BlockSpec((1, n), lambda i: (0, 0))
    two = pl.BlockSpec((2, d), lambda i: (0, 0))
    tok = lambda n: pl.BlockSpec((tm, n), lambda i: (i, 0))
    return pl.pallas_call(
        _mix_out_kernel,
        grid=(t // tm,),
        in_specs=[pl.BlockSpec((2, tm, w), lambda i: (0, i, 0)), tok(w), tok(w), tok(w), tok(d),
                  pl.BlockSpec((d, d), lambda i: (0, 0)), row(w), row(w), two, row(d), two, two,
                  pl.BlockSpec((N_EXPERTS, d), lambda i: (0, 0))],
        out_specs=[tok(d), tok(d), pl.BlockSpec((N_EXPERTS, tm), lambda i: (0, i))],
        out_shape=[jax.ShapeDtypeStruct((t, d), F32), jax.ShapeDtypeStruct((t, d), F32),
                   jax.ShapeDtypeStruct((N_EXPERTS, t), F32)],
        compiler_params=_cparams(("parallel",)),
        name="mix_out",
    )(ya, bonus, g, yb, xa, wo, lnw.reshape(1, w), lnb.reshape(1, w), gt, n2.reshape(1, d), sc, sh, rwt)


def _first_argmax(x, idx, n):
    m = jnp.max(x, axis=0, keepdims=True)
    return m, jnp.min(jnp.where(x == m, idx, n), axis=0, keepdims=True)


def _route_kernel(s_ref, rb_ref, e_out, w_out):
    epg = EXPERTS_PER_GROUP
    tn = s_ref.shape[1]
    idx = lax.broadcasted_iota(jnp.int32, (epg, tn), 0)
    neg = -jnp.inf
    best = None
    for gi in range(N_GROUPS):
        sc = s_ref[gi * epg:(gi + 1) * epg, :]
        bi = sc + rb_ref[gi * epg:(gi + 1) * epg, :]
        m1, i1 = _first_argmax(bi, idx, epg)
        m2 = jnp.max(jnp.where(idx == i1, neg, bi), axis=0, keepdims=True)
        gs = m1 + m2
        if best is None:
            best, g_sel, sc_sel, bi_sel = gs, jnp.zeros((1, tn), jnp.int32), sc, bi
        else:
            better = gs > best
            best = jnp.where(better, gs, best)
            g_sel = jnp.where(better, gi, g_sel)
            sc_sel = jnp.where(better, sc, sc_sel)
            bi_sel = jnp.where(better, bi, bi_sel)
    _, l1 = _first_argmax(bi_sel, idx, epg)
    _, l2 = _first_argmax(jnp.where(idx == l1, neg, bi_sel), idx, epg)
    w1 = jnp.sum(jnp.where(idx == l1, sc_sel, 0.0), axis=0, keepdims=True)
    w2 = jnp.sum(jnp.where(idx == l2, sc_sel, 0.0), axis=0, keepdims=True)
    tot = w1 + w2
    e_out[...] = jnp.concatenate([g_sel * epg + l1, g_sel * epg + l2], axis=0)
    w_out[...] = jnp.concatenate([w1 / tot, w2 / tot], axis=0)


def _route(scores_t, router_b):
    t = scores_t.shape[1]
    tn = _row_tile(t, 1280)
    return pl.pallas_call(
        _route_kernel,
        grid=(t // tn,),
        in_specs=[pl.BlockSpec((N_EXPERTS, tn), lambda i: (0, i)), pl.BlockSpec((N_EXPERTS, 1), lambda i: (0, 0))],
        out_specs=[pl.BlockSpec((TOP_K, tn), lambda i: (0, i)), pl.BlockSpec((TOP_K, tn), lambda i: (0, i))],
        out_shape=[jax.ShapeDtypeStruct((TOP_K, t), jnp.int32), jax.ShapeDtypeStruct((TOP_K, t), F32)],
        compiler_params=_cparams(("parallel",)),
        name="route",
    )(scores_t, router_b.reshape(N_EXPERTS, 1))


def _moe_kernel(be_ref, src_ref, nused_ref, h_hbm, w1_ref, w2_ref, o_hbm, xbuf, ybuf, gsem, ssem):
    del be_ref
    blk = pl.program_id(0)
    base = blk * MOE_BLOCK

    def gather(r):
        tok = jnp.maximum(src_ref[base + r], 0) // TOP_K
        return pltpu.make_async_copy(h_hbm.at[pl.ds(tok, 1)], xbuf.at[pl.ds(r, 1)], gsem)

    def scatter(r):
        slot = jnp.maximum(src_ref[base + r], 0)
        return pltpu.make_async_copy(ybuf.at[pl.ds(r, 1)], o_hbm.at[pl.ds(slot, 1)], ssem)

    @pl.when(blk < nused_ref[0])
    def _():
        def start_gather(r, c):
            gather(r).start()
            return c

        def wait_gather(r, c):
            gather(r).wait()
            return c

        lax.fori_loop(0, MOE_BLOCK, start_gather, 0)
        lax.fori_loop(0, MOE_BLOCK, wait_gather, 0)
        gu = _dot(xbuf[...].astype(BF16), w1_ref[...].astype(BF16))
        gate = gu[:, :D_EXPERT]
        act = gate * jax.nn.sigmoid(gate) * gu[:, D_EXPERT:]
        ybuf[...] = _dot(act.astype(BF16), w2_ref[...].astype(BF16))

        def start_scatter(r, c):
            @pl.when(src_ref[base + r] >= 0)
            def _():
                scatter(r).start()
            return c

        def wait_scatter(r, c):
            @pl.when(src_ref[base + r] >= 0)
            def _():
                scatter(r).wait()
            return c

        lax.fori_loop(0, MOE_BLOCK, start_scatter, 0)
        lax.fori_loop(0, MOE_BLOCK, wait_scatter, 0)


def _moe_experts(h, block_expert, row_src, n_used, w1, w2):
    t, d = h.shape
    n_blocks = block_expert.shape[0]
    grid_spec = pltpu.PrefetchScalarGridSpec(
        num_scalar_prefetch=3,
        grid=(n_blocks,),
        in_specs=[
            pl.BlockSpec(memory_space=pl.ANY),
            pl.BlockSpec((None, d, 2 * D_EXPERT), lambda b, be, src, nu: (be[b], 0, 0)),
            pl.BlockSpec((None, D_EXPERT, d), lambda b, be, src, nu: (be[b], 0, 0)),
        ],
        out_specs=pl.BlockSpec(memory_space=pl.ANY),
        scratch_shapes=[pltpu.VMEM((MOE_BLOCK, d), F32), pltpu.VMEM((MOE_BLOCK, d), F32),
                        pltpu.SemaphoreType.DMA(()), pltpu.SemaphoreType.DMA(())],
    )
    return pl.pallas_call(
        _moe_kernel,
        grid_spec=grid_spec,
        out_shape=jax.ShapeDtypeStruct((t * TOP_K, d), F32),
        compiler_params=_cparams(("arbitrary",)),
        name="moe_experts",
    )(block_expert, row_src, n_used, h, w1, w2)


def _moe_plan(expert):
    t = expert.shape[1]
    m = t * TOP_K
    e_flat = expert.T.reshape(-1)
    order = jnp.argsort(e_flat).astype(jnp.int32)
    e_s = e_flat[order]
    counts = jnp.bincount(e_flat, length=N_EXPERTS).astype(jnp.int32)
    padded = (counts + MOE_BLOCK - 1) // MOE_BLOCK * MOE_BLOCK
    start = jnp.cumsum(counts) - counts
    pend = jnp.cumsum(padded)
    pstart = pend - padded
    dest = pstart[e_s] + jnp.arange(m, dtype=jnp.int32) - start[e_s]
    m_pad = -(-m // MOE_BLOCK) * MOE_BLOCK + N_EXPERTS * MOE_BLOCK
    n_blocks = m_pad // MOE_BLOCK
    row_src = jnp.full((m_pad,), -1, jnp.int32).at[dest].set(order)
    block_start = jnp.arange(n_blocks, dtype=jnp.int32) * MOE_BLOCK
    block_expert = jnp.minimum(jnp.searchsorted(pend, block_start, side='right'), N_EXPERTS - 1).astype(jnp.int32)
    n_used = (pend[-1] // MOE_BLOCK).astype(jnp.int32).reshape(1)
    return block_expert, row_src, n_used


def _combine_kernel(o_ref, w_ref, x_ref, gt_ref, fg_ref, out_ref, *, final, tile_off):
    is_ctx = (pl.program_id(0) + tile_off) == 0
    d = x_ref.shape[1]
    wts = w_ref[...]
    f = o_ref[:, 0:d] * wts[:, 0:1] + o_ref[:, d:2 * d] * wts[:, 1:2]
    gt = jnp.where(is_ctx, gt_ref[1:2, :], gt_ref[0:1, :])
    xn = x_ref[...] + gt * f
    if final:
        xn = _rms(xn) * fg_ref[...]
    out_ref[...] = xn


def _combine(o2, wts, xa, gt, final_g, final):
    t, d = xa.shape
    tm = TOKEN_TILE
    off = 1 if final else 0
    nt = t // tm - off
    return pl.pallas_call(
        functools.partial(_combine_kernel, final=final, tile_off=off),
        grid=(nt,),
        in_specs=[pl.BlockSpec((tm, 2 * d), lambda i: (i + off, 0)), pl.BlockSpec((tm, TOP_K), lambda i: (i + off, 0)),
                  pl.BlockSpec((tm, d), lambda i: (i + off, 0)), pl.BlockSpec((2, d), lambda i: (0, 0)),
                  pl.BlockSpec((1, d), lambda i: (0, 0))],
        out_specs=pl.BlockSpec((tm, d), lambda i: (i, 0)),
        out_shape=jax.ShapeDtypeStruct((nt * tm, d), F32),
        compiler_params=_cparams(("parallel",)),
        name="combine",
    )(o2, wts, xa, gt, final_g.reshape(1, d))


def _rope_tables(n):
    pos = jnp.arange(n)
    half = RET_HEAD // 2
    inv = ROPE_BASE ** (-jnp.arange(0, half, 2, dtype=F32) / half)

    def ang(p):
        a = p.astype(F32)[:, None] * inv
        return jnp.concatenate([a, a], axis=-1)

    a = jnp.concatenate([ang(pos // GRID_W), ang(pos % GRID_W)], axis=-1)
    sign = jnp.tile(jnp.concatenate([-jnp.ones((half // 2,), F32), jnp.ones((half // 2,), F32)]), 2)
    cos = jnp.concatenate([jnp.ones((CTX_LEN, RET_HEAD), F32), jnp.cos(a)], axis=0)
    sin = jnp.concatenate([jnp.zeros((CTX_LEN, RET_HEAD), F32), jnp.sin(a) * sign], axis=0)
    return cos, sin


def _pad_rows(w, rows, offset):
    out = jnp.zeros((rows,) + w.shape[1:], w.dtype)
    return lax.dynamic_update_slice_in_dim(out, w, offset, axis=0)


def kernel(x, c, ctx, c_ctx, ada_w, ada_b, norm1_g, norm2_g, w_in, shift_mu, rwkv_w0, rwkv_w2, rwkv_a0, rwkv_a2,
           rwkv_g2, rwkv_k_k, rwkv_k_a, rwkv_r_k, rwkv_lnx_w, rwkv_lnx_b, ret_log2_decay, ret_gn_w, w_out,
           router_w, router_b, moe_w1, moe_w2, final_g):
    b, n, d = x.shape
    assert b == 1 and d == D_MODEL and ctx.shape[1] == CTX_LEN and n % TOKEN_TILE == 0
    depth = ada_w.shape[0]
    w = RWKV_WIDTH

    xa = jnp.concatenate([ctx[0], x[0]], axis=0)
    cct = jnp.stack([c[0], c_ctx], axis=1)
    mod = _ada_mod(cct, ada_w, ada_b).reshape(depth, 2, 6, d)
    cos, sin = _rope_tables(n)
    rwt = router_w.T
    w1b = moe_w1.astype(BF16)
    w2b = moe_w2.astype(BF16)

    out = None
    for l in range(depth):
        last = l == depth - 1
        sh1, sc1, gt1, sh2, sc2, gt2 = (mod[l, :, i, :] for i in range(6))
        h = _norm_mod(xa, norm1_g[l], sc1, sh1)
        w_pad = jnp.concatenate(
            [w_in[l, :, :RWKV_IN], jnp.zeros((d, RWKV_PAD - RWKV_IN), F32), w_in[l, :, RWKV_IN:]], axis=1)
        p = _matmul(h, w_pad.astype(BF16), BF16)

        mu = jnp.pad(shift_mu[l], ((0, 0), (0, RWKV_PAD - RWKV_IN)))
        w2p = jnp.stack([_pad_rows(rwkv_w2[l, 0], LANE, 0), _pad_rows(rwkv_w2[l, 1], LANE, DECAY_LORA)])
        a2p = jnp.stack([_pad_rows(rwkv_a2[l, 0], LANE, 0), _pad_rows(rwkv_a2[l, 1], LANE, ICLR_LORA)])
        g2p = _pad_rows(rwkv_g2[l], 2 * LANE, 0)
        r, v, kk, g, bonus, lw, kd, bb = _rwkv_prep(
            p, mu, w2p.astype(BF16), a2p.astype(BF16), g2p.astype(BF16), rwkv_w0[l], rwkv_a0[l],
            rwkv_k_k[l].reshape(1, w), rwkv_k_a[l].reshape(1, w), rwkv_r_k[l].reshape(1, w))
        ya = _rwkv_scan(r, v, kk, lw, kd, bb)

        l2d = jnp.broadcast_to(ret_log2_decay[l].astype(F32)[:, :, None, None], (2, RET_HEADS, 1, LANE))
        yb = _retention(p, cos, sin, l2d, ret_gn_w[l])

        xa, h2, scores_t = _mix_out(ya, bonus, g, yb, xa, w_out[l].astype(BF16), rwkv_lnx_w[l], rwkv_lnx_b[l],
                                    gt1, norm2_g[l], sc2, sh2, rwt)
        expert, gate = _route(scores_t, router_b)
        block_expert, row_src, n_used = _moe_plan(expert)
        o2 = _moe_experts(h2, block_expert, row_src, n_used, w1b[l], w2b[l])
        res = _combine(o2.reshape(-1, TOP_K * d), gate.T, xa, gt2, final_g, last)
        if last:
            out = res
        else:
            xa = res
    return out[None]
```

```python
import functools
import math

import jax
import jax.numpy as jnp
from jax import lax
from jax.experimental import pallas as pl
from jax.experimental.pallas import tpu as pltpu

F32 = jnp.float32
BF16 = jnp.bfloat16
HIGHEST = lax.Precision.HIGHEST

D_MODEL = 2048
CTX_LEN = 256
GRID_W = 64
NORM_EPS = 1e-6

RWKV_WIDTH = 1024
RWKV_HEAD = 64
DECAY_LORA = 64
ICLR_LORA = 64
GATE_LORA = 160
LNX_EPS = 64e-5
RWKV_IN = 3 * RWKV_WIDTH + 2 * DECAY_LORA + 2 * ICLR_LORA + GATE_LORA
LORA_PAD = 512
RWKV_PAD = 3 * RWKV_WIDTH + LORA_PAD

RET_WIDTH = 1024
RET_HEAD = 128
RET_HEADS = 8
RET_CHUNK = 128
ROPE_BASE = 10000.0
P_PAD = RWKV_PAD + 4 * RET_WIDTH

N_EXPERTS = 32
N_GROUPS = 4
EXPERTS_PER_GROUP = 8
TOP_K = 2
D_EXPERT = 1024
MOE_BLOCK = 128

LANE = 128
TOKEN_TILE = 256
RWKV_CHUNK = 64
VMEM_LIMIT = 48 * 1024 * 1024


def _cparams(sem):
    return pltpu.CompilerParams(dimension_semantics=sem, vmem_limit_bytes=VMEM_LIMIT)


def _dot(a, b, precision=None):
    return jnp.dot(a, b, preferred_element_type=F32, precision=precision)


def _dot_nt(a, b, precision=None):
    return lax.dot_general(a, b, (((1,), (1,)), ((), ())), preferred_element_type=F32, precision=precision)


def _dot_tn(a, b, precision=None):
    return lax.dot_general(a, b, (((0,), (0,)), ((), ())), preferred_element_type=F32, precision=precision)


def _split_bf16(x):
    hi = x.astype(BF16)
    lo = (x - hi.astype(F32)).astype(BF16)
    return hi, lo


def _split3_bf16(x):
    hi = x.astype(BF16)
    rest = x - hi.astype(F32)
    mid = rest.astype(BF16)
    return hi, mid, (rest - mid.astype(F32)).astype(BF16)


def _head_sum(x, head):
    ri = lax.broadcasted_iota(jnp.int32, (LANE, LANE), 0) // head
    ci = lax.broadcasted_iota(jnp.int32, (LANE, LANE), 1) // head
    bd = (ri == ci).astype(BF16)
    outs = []
    for j in range(x.shape[1] // LANE):
        hi, lo = _split_bf16(x[:, j * LANE:(j + 1) * LANE])
        outs.append(_dot(hi, bd) + _dot(lo, bd))
    return outs[0] if len(outs) == 1 else jnp.concatenate(outs, axis=1)


def _ada_kernel(cct_ref, w_ref, b_ref, o_ref):
    k = pl.program_id(2)

    @pl.when(k == 0)
    def _():
        o_ref[0] = jnp.broadcast_to(b_ref[0], o_ref.shape[1:])

    s = cct_ref[...]
    s = s * jax.nn.sigmoid(s)
    w = w_ref[0]
    acc0 = jnp.sum(s[:, 0:1] * w, axis=0, keepdims=True)
    acc1 = jnp.sum(s[:, 1:2] * w, axis=0, keepdims=True)
    o_ref[0] += jnp.concatenate([acc0, acc1], axis=0)


def _ada_mod(cct, ada_w, ada_b):
    depth, d, n6 = ada_w.shape
    tk, tn = 512, 2048
    return pl.pallas_call(
        _ada_kernel,
        grid=(depth, n6 // tn, d // tk),
        in_specs=[
            pl.BlockSpec((tk, 2), lambda l, j, k: (k, 0)),
            pl.BlockSpec((1, tk, tn), lambda l, j, k: (l, k, j)),
            pl.BlockSpec((1, 1, tn), lambda l, j, k: (l, 0, j)),
        ],
        out_specs=pl.BlockSpec((1, 2, tn), lambda l, j, k: (l, 0, j)),
        out_shape=jax.ShapeDtypeStruct((depth, 2, n6), F32),
        compiler_params=_cparams(("parallel", "parallel", "arbitrary")),
        name="ada_mod",
    )(cct, ada_w, ada_b.reshape(depth, 1, n6))


def _rms(x):
    return x * lax.rsqrt(jnp.mean(x * x, axis=-1, keepdims=True) + NORM_EPS)


def _norm_mod_kernel(x_ref, g_ref, sc_ref, sh_ref, o_ref):
    is_ctx = pl.program_id(0) == 0
    y = _rms(x_ref[...]) * g_ref[...]
    sc = jnp.where(is_ctx, sc_ref[1:2, :], sc_ref[0:1, :])
    sh = jnp.where(is_ctx, sh_ref[1:2, :], sh_ref[0:1, :])
    o_ref[...] = (y * (1.0 + sc) + sh).astype(o_ref.dtype)


def _norm_mod(xa, g, sc, sh):
    t, d = xa.shape
    row = pl.BlockSpec((1, d), lambda i: (0, 0))
    two = pl.BlockSpec((2, d), lambda i: (0, 0))
    return pl.pallas_call(
        _norm_mod_kernel,
        grid=(t // TOKEN_TILE,),
        in_specs=[pl.BlockSpec((TOKEN_TILE, d), lambda i: (i, 0)), row, two, two],
        out_specs=pl.BlockSpec((TOKEN_TILE, d), lambda i: (i, 0)),
        out_shape=jax.ShapeDtypeStruct((t, d), BF16),
        compiler_params=_cparams(("parallel",)),
        name="norm_mod",
    )(xa, g.reshape(1, d), sc, sh)


def _matmul_kernel(a_ref, b_ref, o_ref):
    o_ref[...] = _dot(a_ref[...], b_ref[...]).astype(o_ref.dtype)


def _row_tile(t, cap):
    tm = TOKEN_TILE
    for cand in range(TOKEN_TILE, cap + 1, TOKEN_TILE):
        if t % cand == 0:
            tm = cand
    return tm


def _matmul(a, b, out_dtype):
    m, k = a.shape
    n = b.shape[1]
    tm, tn = _row_tile(m, 1280), 512
    return pl.pallas_call(
        _matmul_kernel,
        grid=(m // tm, n // tn),
        in_specs=[pl.BlockSpec((tm, k), lambda i, j: (i, 0)), pl.BlockSpec((k, tn), lambda i, j: (0, j))],
        out_specs=pl.BlockSpec((tm, tn), lambda i, j: (i, j)),
        out_shape=jax.ShapeDtypeStruct((m, n), out_dtype),
        compiler_params=_cparams(("parallel", "arbitrary")),
        name="in_proj",
    )(a, b)


def _rwkv_prep_kernel(pm_ref, pp_ref, pn_ref, mu_ref, w2_ref, a2_ref, g2_ref, w0_ref, a0_ref, kk_ref, ka_ref,
                      rk_ref, r_out, v_out, kk_out, g_out, bonus_out, lw_out, kd_out, b_out):
    i = pl.program_id(0)
    nt = pl.num_programs(0)
    is_ctx = i == 0
    tm = TOKEN_TILE
    row = lax.broadcasted_iota(jnp.int32, (tm, 1), 0)
    in_row = row & (GRID_W - 1)
    edge_lo = jnp.where(is_ctx, row, in_row) == 0
    edge_hi = jnp.where(is_ctx, row - (tm - 1), in_row - (GRID_W - 1)) == 0
    no_up = jnp.logical_or(is_ctx, i == 1)
    no_down = jnp.logical_or(is_ctx, i == nt - 1)
    x_gate = jnp.where(is_ctx, 0.0, 1.0)

    def shifted(c0, c1):
        p = pm_ref[:, c0:c1].astype(F32)
        mu = mu_ref[:, c0:c1]
        prev = jnp.where(edge_lo, 0.0, pltpu.roll(p, 1, axis=0))
        nxt = jnp.where(edge_hi, 0.0, pltpu.roll(p, tm - 1, axis=0))
        out = p + mu[0:1] * (prev - p) + mu[1:2] * (nxt - p)
        above = jnp.where(no_up, 0.0, pp_ref[:, c0:c1].astype(F32))
        below = jnp.where(no_down, 0.0, pn_ref[:, c0:c1].astype(F32))
        up = jnp.concatenate([above, p[:tm - GRID_W]], axis=0)
        down = jnp.concatenate([p[GRID_W:], below], axis=0)
        return out + x_gate * (mu[2:3] * (up - p) + mu[3:4] * (down - p))

    w = RWKV_WIDTH
    ul = shifted(3 * w, 3 * w + LORA_PAD)
    twd = jnp.tanh(ul[:, 0:LANE]).astype(BF16)
    ad = ul[:, LANE:2 * LANE].astype(BF16)
    sg = jax.nn.sigmoid(ul[:, 2 * LANE:4 * LANE]).astype(BF16)
    g_out[...] = _dot(sg, g2_ref[...]).astype(g_out.dtype)

    r = shifted(0, w)
    k = shifted(w, 2 * w)
    v = shifted(2 * w, 3 * w)
    r_out[...] = r.astype(r_out.dtype)
    v_out[...] = v.astype(v_out.dtype)
    kkr = k * kk_ref[...]
    nrm = jnp.sqrt(_head_sum(kkr * kkr, RWKV_HEAD))
    kk = kkr / jnp.maximum(nrm, 1e-12)
    kk_out[...] = kk.astype(kk_out.dtype)
    ksum = jnp.zeros_like(k)
    for d in range(2):
        z = w0_ref[d:d + 1, :] + _dot(twd, w2_ref[d])
        lw_out[d] = -jax.nn.sigmoid(z) * math.exp(-0.5)
        a = jax.nn.sigmoid(a0_ref[d:d + 1, :] + _dot(ad, a2_ref[d]))
        kd = k * (1.0 + (a - 1.0) * ka_ref[...])
        kd_out[d] = kd.astype(kd_out.dtype)
        b_out[d] = (a * kk).astype(b_out.dtype)
        ksum = ksum + kd
    bonus = _head_sum(r * ksum * rk_ref[...], RWKV_HEAD) * v
    bonus_out[...] = bonus.astype(bonus_out.dtype)


def _rwkv_prep(p, mu, w2p, a2p, g2p, w0, a0, k_k, k_a, r_k):
    t = p.shape[0]
    nt = t // TOKEN_TILE
    per_tile = TOKEN_TILE // GRID_W
    n_rows = t // GRID_W
    w = RWKV_WIDTH
    full = lambda shape: pl.BlockSpec(shape, lambda i: (0,) * len(shape))
    tok = pl.BlockSpec((TOKEN_TILE, w), lambda i: (i, 0))
    tok2 = pl.BlockSpec((2, TOKEN_TILE, w), lambda i: (0, i, 0))
    one = jax.ShapeDtypeStruct((t, w), BF16)
    return pl.pallas_call(
        _rwkv_prep_kernel,
        grid=(nt,),
        in_specs=[
            pl.BlockSpec((TOKEN_TILE, RWKV_PAD), lambda i: (i, 0)),
            pl.BlockSpec((GRID_W, RWKV_PAD), lambda i: (jnp.maximum(i * per_tile - 1, 0), 0)),
            pl.BlockSpec((GRID_W, RWKV_PAD), lambda i: (jnp.minimum((i + 1) * per_tile, n_rows - 1), 0)),
            full((4, RWKV_PAD)), full((2, LANE, w)), full((2, LANE, w)), full((2 * LANE, w)),
            full((2, w)), full((2, w)), full((1, w)), full((1, w)), full((1, w)),
        ],
        out_specs=[tok, tok, tok, tok, tok, tok2, tok2, tok2],
        out_shape=[one, one, one, one, one,
                   jax.ShapeDtypeStruct((2, t, w), F32),
                   jax.ShapeDtypeStruct((2, t, w), BF16),
                   jax.ShapeDtypeStruct((2, t, w), BF16)],
        compiler_params=_cparams(("parallel",)),
        name="rwkv_prep",
    )(p, p, p, mu, w2p, a2p, g2p, w0, a0, k_k, k_a, r_k)


def _stack_heads(x):
    lane = lax.broadcasted_iota(jnp.int32, (1, LANE), 1)
    zero = jnp.zeros_like(x)
    return jnp.concatenate([jnp.where(lane < RWKV_HEAD, x, zero), jnp.where(lane >= RWKV_HEAD, x, zero)], axis=0)


def _rwkv_scan_kernel(r_ref, v_ref, kk_ref, lw_ref, kd_ref, b_ref, y_ref, m_ref, q_s, yl_s, pm_s, ml_s, el_s):
    d = pl.program_id(0)
    s = pl.program_id(2)
    sign = 1 - 2 * d
    c = RWKV_CHUNK
    tm = TOKEN_TILE
    n_chunks = tm // c
    n_pairs = r_ref.shape[1] // LANE

    @pl.when(s == 0)
    def _():
        m_ref[...] = jnp.zeros_like(m_ref)

    ri = lax.broadcasted_iota(jnp.int32, (tm, tm), 0)
    ci = lax.broadcasted_iota(jnp.int32, (tm, tm), 1)
    same_chunk = (ri // c) == (ci // c)
    along = ((ci - ri) * sign) <= 0
    tri = jnp.where(jnp.logical_and(same_chunk, along), 1.0, 0.0).astype(BF16)
    lw = lw_ref[...]
    lw3 = _split3_bf16(lw)
    sums = sum(_dot(jnp.concatenate([tri, jnp.where(same_chunk, 1.0, 0.0).astype(BF16)], axis=0), piece)
               for piece in lw3)
    lc = sums[:tm]
    lend = sums[tm:]
    e_neg = jnp.exp(-lc)
    e_end = jnp.exp(lend - lc)
    kd = kd_ref[...].astype(F32)
    b = b_ref[...].astype(F32)
    rl_all = r_ref[...].astype(F32) * jnp.exp(lc)
    kke_all = (kk_ref[...].astype(F32) * jnp.exp(lc - lw)).astype(BF16)
    kinv_all = (kd * e_neg).astype(BF16)
    binv_all = (b * e_neg).astype(BF16)
    kdec_all = (kd * e_end).astype(BF16)
    bdec_all = (b * e_end).astype(BF16)

    trow = lax.broadcasted_iota(jnp.int32, (c, LANE), 0)
    tcol = lax.broadcasted_iota(jnp.int32, (c, LANE), 1) & (RWKV_HEAD - 1)
    rel = (tcol - trow) * sign
    incl = rel <= 0
    strict = rel < 0
    eye = jnp.where(rel == 0, 1.0, 0.0)
    hi = lax.broadcasted_iota(jnp.int32, (LANE, LANE), 0) < RWKV_HEAD
    hj = lax.broadcasted_iota(jnp.int32, (LANE, LANE), 1) < RWKV_HEAD
    same_head = hi == hj
    ones = jnp.ones((c, LANE), BF16)

    units = [(cc, pr) for cc in range(n_chunks) for pr in range(n_pairs)]

    def sl(x, u):
        cc, pr = u
        return x[cc * c:(cc + 1) * c, pr * LANE:(pr + 1) * LANE]

    rl = [sl(rl_all, u) for u in units]
    kke = [sl(kke_all, u) for u in units]
    vb = [v_ref[u[0] * c:(u[0] + 1) * c, u[1] * LANE:(u[1] + 1) * LANE] for u in units]
    vst = [_stack_heads(x) for x in vb]
    lhs = [jnp.concatenate([kke[i], rl[i].astype(BF16)], axis=0) for i in range(len(units))]
    ak = [_dot_nt(lhs[i], _stack_heads(sl(kinv_all, u))) for i, u in enumerate(units)]
    ab = [_dot_nt(lhs[i], _stack_heads(sl(binv_all, u))) for i, u in enumerate(units)]
    a_kk = [jnp.where(strict, x[:c], 0.0).astype(BF16) for x in ak]
    a_rk = [jnp.where(incl, x[c:], 0.0).astype(BF16) for x in ak]
    a_rb = [jnp.where(incl, x[c:], 0.0).astype(BF16) for x in ab]
    pw = [jnp.where(strict, -x[:c], 0.0) for x in ab]
    tinv = [eye + x for x in pw]
    for _ in range(5):
        pwb = [x.astype(BF16) for x in pw]
        pw = [_dot(x, _stack_heads(x)) for x in pwb]
        tinv = [t + _dot(t.astype(BF16), _stack_heads(x.astype(BF16))) for t, x in zip(tinv, pw)]
    tb = [t.astype(BF16) for t in tinv]
    w1 = [_dot(t, _stack_heads(x)).astype(BF16) for t, x in zip(tb, kke)]
    av = [_dot(a, x).astype(BF16) for a, x in zip(a_kk, vst)]
    w2 = [_dot(t, _stack_heads(x)).astype(BF16) for t, x in zip(tb, av)]
    for i, u in enumerate(units):
        cc, pr = u
        bdec = sl(bdec_all, u)
        q_s[cc, pr] = rl[i] - _dot(a_rb[i], _stack_heads(w1[i]))
        yl_s[cc, pr] = _dot(a_rk[i], vst[i]) - _dot(a_rb[i], _stack_heads(w2[i]))
        pm_s[cc, pr] = jnp.where(same_head, -_dot_tn(bdec, w1[i]), 0.0)
        ml_s[cc, pr] = jnp.where(same_head, _dot_tn(sl(kdec_all, u), vb[i]) - _dot_tn(bdec, w2[i]), 0.0)
        el_s[cc, pr] = jnp.exp(sum(_dot_tn(sl(piece, u), ones) for piece in lw3))

    m = [m_ref[pr] for pr in range(n_pairs)]
    for step in range(n_chunks):
        cc = jnp.where(d == 1, n_chunks - 1 - step, step)
        rows = pl.ds(pl.multiple_of(cc * c, c), c)
        for pr in range(n_pairs):
            l_hi, l_lo = _split_bf16(jnp.concatenate([q_s[cc, pr], pm_s[cc, pr]], axis=0))
            m_hi, m_lo = _split_bf16(m[pr])
            res = _dot(l_hi, m_hi) + _dot(l_hi, m_lo) + _dot(l_lo, m_hi)
            y_ref[rows, pr * LANE:(pr + 1) * LANE] = (res[:c] + yl_s[cc, pr]).astype(y_ref.dtype)
            m[pr] = el_s[cc, pr] * m[pr] + res[c:] + ml_s[cc, pr]
    for pr in range(n_pairs):
        m_ref[pr] = m[pr]


def _rwkv_scan(r, v, kk, lw, kd, b):
    t, w = r.shape
    nt = t // TOKEN_TILE
    wb = 2 * LANE
    n_chunks = TOKEN_TILE // RWKV_CHUNK
    n_pairs = wb // LANE

    def tile(d, s):
        return jnp.where(jnp.logical_and(d == 1, s > 0), nt - s, s)

    shared = pl.BlockSpec((TOKEN_TILE, wb), lambda d, j, s: (tile(d, s), j))
    per_dir = pl.BlockSpec((None, TOKEN_TILE, wb), lambda d, j, s: (d, tile(d, s), j))
    unit = lambda rows: pltpu.VMEM((n_chunks, n_pairs, rows, LANE), F32)
    return pl.pallas_call(
        _rwkv_scan_kernel,
        grid=(2, w // wb, nt),
        in_specs=[shared, shared, shared, per_dir, per_dir, per_dir],
        out_specs=per_dir,
        out_shape=jax.ShapeDtypeStruct((2, t, w), BF16),
        scratch_shapes=[pltpu.VMEM((n_pairs, LANE, LANE), F32), unit(RWKV_CHUNK), unit(RWKV_CHUNK), unit(LANE),
                        unit(LANE), unit(LANE)],
        compiler_params=_cparams(("parallel", "parallel", "arbitrary")),
        name="rwkv_scan",
    )(r, v, kk, lw, kd, b)


def _rotate(t, cos, sin_signed):
    lane = lax.broadcasted_iota(jnp.int32, (1, LANE), 1)
    first = (lane & 63) < 32
    swapped = jnp.where(first, pltpu.roll(t, LANE - 32, axis=1), pltpu.roll(t, 32, axis=1))
    return t * cos + swapped * sin_signed


RET_TILE_HEADS = 2


def _ret_qk(q_ref, k_ref, cos_ref, sin_ref, rows, lanes):
    cos = cos_ref[rows, :]
    sin = sin_ref[rows, :]
    q = _rotate(q_ref[rows, lanes].astype(F32), cos, sin)
    k = _rotate(k_ref[rows, lanes].astype(F32) * (RET_HEAD ** -0.5), cos, sin)
    return q, k


def _log_gamma(l2d_ref, hh):
    lg = jnp.log1p(-jnp.exp2(-l2d_ref[hh:hh + 1, :]))
    return jnp.broadcast_to(lg, (RET_CHUNK, LANE))


def _ret_units(reverse):
    chunks = range(TOKEN_TILE // RET_CHUNK)
    for hh in range(RET_TILE_HEADS):
        for cc in (reversed(chunks) if reverse else chunks):
            yield hh, slice(cc * RET_CHUNK, (cc + 1) * RET_CHUNK), slice(hh * LANE, (hh + 1) * LANE)


def _ret_fwd_kernel(q_ref, k_ref, v_ref, cos_ref, sin_ref, l2d_ref, y_ref, s_ref):
    @pl.when(pl.program_id(1) == 0)
    def _():
        s_ref[...] = jnp.zeros_like(s_ref)

    c = RET_CHUNK
    idx = lax.broadcasted_iota(jnp.int32, (c, LANE), 0).astype(F32)
    for hh, rows, lanes in _ret_units(False):
        q, k = _ret_qk(q_ref, k_ref, cos_ref, sin_ref, rows, lanes)
        lg = _log_gamma(l2d_ref, hh)
        s = s_ref[hh]
        y_ref[rows, lanes] = _dot((q * jnp.exp((idx + 1.0) * lg)).astype(BF16), s.astype(BF16)).astype(y_ref.dtype)
        kz = (k * jnp.exp((c - 1.0 - idx) * lg)).astype(BF16)
        s_ref[hh] = s * jnp.exp(c * lg) + _dot_tn(kz, v_ref[rows, lanes])


def _ret_bwd_kernel(q_ref, k_ref, v_ref, g_ref, y1_ref, cos_ref, sin_ref, l2f_ref, l2b_ref, gn_ref, o_ref, s_ref):
    @pl.when(pl.program_id(1) == 0)
    def _():
        s_ref[...] = jnp.zeros_like(s_ref)

    c = RET_CHUNK
    rowi = lax.broadcasted_iota(jnp.int32, (c, LANE), 0)
    coli = lax.broadcasted_iota(jnp.int32, (c, LANE), 1)
    diff = (rowi - coli).astype(F32)
    idx = rowi.astype(F32)
    for hh, rows, lanes in _ret_units(True):
        q, k = _ret_qk(q_ref, k_ref, cos_ref, sin_ref, rows, lanes)
        v = v_ref[rows, lanes]
        lgf = _log_gamma(l2f_ref, hh)
        lgb = _log_gamma(l2b_ref, hh)
        dmat = jnp.where(rowi >= coli, jnp.exp(diff * lgf), jnp.exp(-diff * lgb))
        scores = (_dot_nt(q.astype(BF16), k.astype(BF16)) * dmat).astype(BF16)
        s = s_ref[hh]
        y = (y1_ref[rows, lanes].astype(F32) + _dot(scores, v)
             + _dot((q * jnp.exp((c - idx) * lgb)).astype(BF16), s.astype(BF16)))
        s_ref[hh] = s * jnp.exp(c * lgb) + _dot_tn((k * jnp.exp(idx * lgb)).astype(BF16), v)
        yn = y * lax.rsqrt(jnp.mean(y * y, axis=-1, keepdims=True) + NORM_EPS)
        g = g_ref[rows, lanes].astype(F32)
        o_ref[rows, lanes] = (yn * gn_ref[:, lanes] * (g * jax.nn.sigmoid(g))).astype(o_ref.dtype)


def _retention(p, cos, sin, l2d, gn_w):
    t = p.shape[0]
    nt = t // TOKEN_TILE
    wb = RET_TILE_HEADS * LANE
    n_col = RET_WIDTH // wb
    base = RWKV_PAD // wb
    l2d = l2d.reshape(2, n_col, RET_TILE_HEADS, LANE)

    def rtile(s):
        return jnp.where(s > 0, nt - s, s)

    blk = (TOKEN_TILE, wb)
    rope = (TOKEN_TILE, LANE)
    dec = lambda d: pl.BlockSpec((None, None, RET_TILE_HEADS, LANE), lambda j, s: (d, j, 0, 0))
    col = lambda part: pl.BlockSpec(blk, lambda j, s: (s, base + part * n_col + j))
    rcol = lambda part: pl.BlockSpec(blk, lambda j, s: (rtile(s), base + part * n_col + j))
    scratch = [pltpu.VMEM((RET_TILE_HEADS, RET_HEAD, RET_HEAD), F32)]
    y1 = pl.pallas_call(
        _ret_fwd_kernel,
        grid=(n_col, nt),
        in_specs=[col(0), col(1), col(2), pl.BlockSpec(rope, lambda j, s: (s, 0)),
                  pl.BlockSpec(rope, lambda j, s: (s, 0)), dec(0)],
        out_specs=pl.BlockSpec(blk, lambda j, s: (s, j)),
        out_shape=jax.ShapeDtypeStruct((t, RET_WIDTH), BF16),
        scratch_shapes=scratch,
        compiler_params=_cparams(("parallel", "arbitrary")),
        name="ret_fwd",
    )(p, p, p, cos, sin, l2d)
    return pl.pallas_call(
        _ret_bwd_kernel,
        grid=(n_col, nt),
        in_specs=[rcol(0), rcol(1), rcol(2), rcol(3), pl.BlockSpec(blk, lambda j, s: (rtile(s), j)),
                  pl.BlockSpec(rope, lambda j, s: (rtile(s), 0)), pl.BlockSpec(rope, lambda j, s: (rtile(s), 0)),
                  dec(0), dec(1), pl.BlockSpec((1, wb), lambda j, s: (0, j))],
        out_specs=pl.BlockSpec(blk, lambda j, s: (rtile(s), j)),
        out_shape=jax.ShapeDtypeStruct((t, RET_WIDTH), BF16),
        scratch_shapes=scratch,
        compiler_params=_cparams(("parallel", "arbitrary")),
        name="ret_bwd",
    )(p, p, p, p, y1, cos, sin, l2d, l2d, gn_w.reshape(1, RET_WIDTH))


def _mix_out_kernel(ya_ref, bonus_ref, g_ref, yb_ref, x_ref, wo_ref, lnw_ref, lnb_ref, gt_ref, n2_ref, sc_ref,
                    sh_ref, rw_ref, x_out, h_out, s_out):
    is_ctx = pl.program_id(0) == 0
    y = ya_ref[0].astype(F32) + ya_ref[1].astype(F32)
    inv_n = 1.0 / RWKV_HEAD
    mu = _head_sum(y, RWKV_HEAD) * inv_n
    yc = y - mu
    var = _head_sum(yc * yc, RWKV_HEAD) * inv_n
    yn = yc * lax.rsqrt(var + LNX_EPS)
    ya = (yn * lnw_ref[...] + lnb_ref[...] + bonus_ref[...].astype(F32)) * g_ref[...].astype(F32)
    z = _dot(ya.astype(BF16), wo_ref[0:RWKV_WIDTH, :]) + _dot(yb_ref[...], wo_ref[RWKV_WIDTH:, :])
    pick = lambda ref: jnp.where(is_ctx, ref[1:2, :], ref[0:1, :])
    xn = x_ref[...] + pick(gt_ref) * z
    x_out[...] = xn
    h = _rms(xn) * n2_ref[...] * (1.0 + pick(sc_ref)) + pick(sh_ref)
    h_out[...] = h
    s_out[...] = jax.nn.sigmoid(_dot_nt(rw_ref[...], h, precision=HIGHEST))


def _mix_out(ya, bonus, g, yb, xa, wo, lnw, lnb, gt, n2, sc, sh, rwt):
    t, d = xa.shape
    w = RWKV_WIDTH
    tm = TOKEN_TILE
    row = lambda n: pl.BlockSpec((1, n), lambda i: (0, 0))
    two = pl.BlockSpec((2, d), lambda i: (0, 0))
    tok = lambda n: pl.BlockSpec((tm, n), lambda i: (i, 0))
    return pl.pallas_call(
        _mix_out_kernel,
        grid=(t // tm,),
        in_specs=[pl.BlockSpec((2, tm, w), lambda i: (0, i, 0)), tok(w), tok(w), tok(w), tok(d),
                  pl.BlockSpec((d, d), lambda i: (0, 0)), row(w), row(w), two, row(d), two, two,
                  pl.BlockSpec((N_EXPERTS, d), lambda i: (0, 0))],
        out_specs=[tok(d), tok(d), pl.BlockSpec((N_EXPERTS, tm), lambda i: (0, i))],
        out_shape=[jax.ShapeDtypeStruct((t, d), F32), jax.ShapeDtypeStruct((t, d), F32),
                   jax.ShapeDtypeStruct((N_EXPERTS, t), F32)],
        compiler_params=_cparams(("parallel",)),
        name="mix_out",
    )(ya, bonus, g, yb, xa, wo, lnw.reshape(1, w), lnb.reshape(1, w), gt, n2.reshape(1, d), sc, sh, rwt)


def _first_argmax(x, idx, n):
    m = jnp.max(x, axis=0, keepdims=True)
    return m, jnp.min(jnp.where(x == m, idx, n), axis=0, keepdims=True)


def _route_kernel(s_ref, rb_ref, e_out, w_out):
    epg = EXPERTS_PER_GROUP
    tn = s_ref.shape[1]
    idx = lax.broadcasted_iota(jnp.int32, (epg, tn), 0)
    neg = -jnp.inf
    best = None
    for gi in range(N_GROUPS):
        sc = s_ref[gi * epg:(gi + 1) * epg, :]
        bi = sc + rb_ref[gi * epg:(gi + 1) * epg, :]
        m1, i1 = _first_argmax(bi, idx, epg)
        m2 = jnp.max(jnp.where(idx == i1, neg, bi), axis=0, keepdims=True)
        gs = m1 + m2
        if best is None:
            best, g_sel, sc_sel, bi_sel = gs, jnp.zeros((1, tn), jnp.int32), sc, bi
        else:
            better = gs > best
            best = jnp.where(better, gs, best)
            g_sel = jnp.where(better, gi, g_sel)
            sc_sel = jnp.where(better, sc, sc_sel)
            bi_sel = jnp.where(better, bi, bi_sel)
    _, l1 = _first_argmax(bi_sel, idx, epg)
    _, l2 = _first_argmax(jnp.where(idx == l1, neg, bi_sel), idx, epg)
    w1 = jnp.sum(jnp.where(idx == l1, sc_sel, 0.0), axis=0, keepdims=True)
    w2 = jnp.sum(jnp.where(idx == l2, sc_sel, 0.0), axis=0, keepdims=True)
    tot = w1 + w2
    e_out[...] = jnp.concatenate([g_sel * epg + l1, g_sel * epg + l2], axis=0)
    w_out[...] = jnp.concatenate([w1 / tot, w2 / tot], axis=0)


def _route(scores_t, router_b):
    t = scores_t.shape[1]
    tn = _row_tile(t, 1280)
    return pl.pallas_call(
        _route_kernel,
        grid=(t // tn,),
        in_specs=[pl.BlockSpec((N_EXPERTS, tn), lambda i: (0, i)), pl.BlockSpec((N_EXPERTS, 1), lambda i: (0, 0))],
        out_specs=[pl.BlockSpec((TOP_K, tn), lambda i: (0, i)), pl.BlockSpec((TOP_K, tn), lambda i: (0, i))],
        out_shape=[jax.ShapeDtypeStruct((TOP_K, t), jnp.int32), jax.ShapeDtypeStruct((TOP_K, t), F32)],
        compiler_params=_cparams(("parallel",)),
        name="route",
    )(scores_t, router_b.reshape(N_EXPERTS, 1))


def _moe_kernel(be_ref, src_ref, nvalid_ref, nused_ref, h_hbm, w1_ref, w2_ref, o_hbm, xbuf, ybuf, gsem, ssem):
    del be_ref
    blk = pl.program_id(0)
    n_used = nused_ref[0]
    d = xbuf.shape[2]
    slot = blk % 2

    def issue_gather(b, sl):
        def body(r, carry):
            tok = src_ref[b * MOE_BLOCK + r] // TOP_K
            pltpu.make_async_copy(h_hbm.at[pl.ds(tok, 1)], xbuf.at[sl, pl.ds(r, 1)], gsem.at[sl]).start()
            return carry

        lax.fori_loop(0, MOE_BLOCK, body, 0, unroll=8)

    def scatter(b, r):
        src = src_ref[b * MOE_BLOCK + r]
        lane0 = pl.multiple_of((src % TOP_K) * d, d)
        return pltpu.make_async_copy(ybuf.at[pl.ds(r, 1)], o_hbm.at[pl.ds(src // TOP_K, 1), pl.ds(lane0, d)], ssem)

    def wait_scatter(b):
        def body(r, carry):
            scatter(b, r).wait()
            return carry

        lax.fori_loop(0, nvalid_ref[b], body, 0)

    @pl.when(blk < n_used)
    def _():
        @pl.when(blk == 0)
        def _():
            issue_gather(blk, slot)

        @pl.when(blk + 1 < n_used)
        def _():
            issue_gather(blk + 1, 1 - slot)

        pltpu.make_async_copy(h_hbm.at[pl.ds(0, MOE_BLOCK)], xbuf.at[slot], gsem.at[slot]).wait()
        gu = _dot(xbuf[slot].astype(BF16), w1_ref[...])
        gate = gu[:, :D_EXPERT]
        act = (gate * jax.nn.sigmoid(gate) * gu[:, D_EXPERT:]).astype(BF16)

        @pl.when(blk > 0)
        def _():
            wait_scatter(blk - 1)

        ybuf[...] = _dot(act, w2_ref[...])

        def start_scatter(r, carry):
            scatter(blk, r).start()
            return carry

        lax.fori_loop(0, nvalid_ref[blk], start_scatter, 0)

        @pl.when(blk == n_used - 1)
        def _():
            wait_scatter(blk)


def _moe_experts(h, block_expert, row_src, n_valid, n_used, w1, w2):
    t, d = h.shape
    n_blocks = block_expert.shape[0]
    grid_spec = pltpu.PrefetchScalarGridSpec(
        num_scalar_prefetch=4,
        grid=(n_blocks,),
        in_specs=[
            pl.BlockSpec(memory_space=pl.ANY),
            pl.BlockSpec((None, d, 2 * D_EXPERT), lambda b, be, src, nv, nu: (be[b], 0, 0)),
            pl.BlockSpec((None, D_EXPERT, d), lambda b, be, src, nv, nu: (be[b], 0, 0)),
        ],
        out_specs=pl.BlockSpec(memory_space=pl.ANY),
        scratch_shapes=[pltpu.VMEM((2, MOE_BLOCK, d), F32), pltpu.VMEM((MOE_BLOCK, d), F32),
                        pltpu.SemaphoreType.DMA((2,)), pltpu.SemaphoreType.DMA(())],
    )
    return pl.pallas_call(
        _moe_kernel,
        grid_spec=grid_spec,
        out_shape=jax.ShapeDtypeStruct((t, TOP_K * d), F32),
        compiler_params=_cparams(("arbitrary",)),
        name="moe_experts",
    )(block_expert, row_src, n_valid, n_used, h, w1, w2)


def _moe_plan(expert):
    t = expert.shape[1]
    m = t * TOP_K
    e_flat = expert.T.reshape(-1)
    onehot = (e_flat[:, None] == jnp.arange(N_EXPERTS, dtype=jnp.int32)[None, :]).astype(jnp.int32)
    csum = jnp.cumsum(onehot, axis=0)
    counts = csum[-1]
    padded = (counts + MOE_BLOCK - 1) // MOE_BLOCK * MOE_BLOCK
    pend = jnp.cumsum(padded)
    pstart = pend - padded
    dest = jnp.sum(onehot * (pstart[None, :] + csum - 1), axis=1)
    m_pad = -(-m // MOE_BLOCK) * MOE_BLOCK + N_EXPERTS * MOE_BLOCK
    n_blocks = m_pad // MOE_BLOCK
    row_src = jnp.zeros((m_pad,), jnp.int32).at[dest].set(jnp.arange(m, dtype=jnp.int32))
    block_start = jnp.arange(n_blocks, dtype=jnp.int32) * MOE_BLOCK
    owner = jnp.sum((block_start[:, None] >= pend[None, :]).astype(jnp.int32), axis=1)
    block_expert = jnp.minimum(owner, N_EXPERTS - 1)
    real_end = jnp.sum(jnp.where(jnp.arange(N_EXPERTS)[None, :] == block_expert[:, None],
                                 (pstart + counts)[None, :], 0), axis=1)
    n_valid = jnp.clip(real_end - block_start, 0, MOE_BLOCK).astype(jnp.int32)
    n_used = (pend[-1] // MOE_BLOCK).astype(jnp.int32).reshape(1)
    return block_expert.astype(jnp.int32), row_src, n_valid, n_used


def _combine_kernel(o_ref, w_ref, x_ref, gt_ref, fg_ref, out_ref, *, final, tile_off):
    is_ctx = (pl.program_id(0) + tile_off) == 0
    d = x_ref.shape[1]
    wts = w_ref[...]
    f = o_ref[:, 0:d] * wts[:, 0:1] + o_ref[:, d:2 * d] * wts[:, 1:2]
    gt = jnp.where(is_ctx, gt_ref[1:2, :], gt_ref[0:1, :])
    xn = x_ref[...] + gt * f
    if final:
        xn = _rms(xn) * fg_ref[...]
    out_ref[...] = xn


def _combine(o2, wts, xa, gt, final_g, final):
    t, d = xa.shape
    tm = TOKEN_TILE
    off = 1 if final else 0
    nt = t // tm - off
    return pl.pallas_call(
        functools.partial(_combine_kernel, final=final, tile_off=off),
        grid=(nt,),
        in_specs=[pl.BlockSpec((tm, 2 * d), lambda i: (i + off, 0)), pl.BlockSpec((tm, TOP_K), lambda i: (i + off, 0)),
                  pl.BlockSpec((tm, d), lambda i: (i + off, 0)), pl.BlockSpec((2, d), lambda i: (0, 0)),
                  pl.BlockSpec((1, d), lambda i: (0, 0))],
        out_specs=pl.BlockSpec((tm, d), lambda i: (i, 0)),
        out_shape=jax.ShapeDtypeStruct((nt * tm, d), F32),
        compiler_params=_cparams(("parallel",)),
        name="combine",
    )(o2, wts, xa, gt, final_g.reshape(1, d))


def _rope_tables(n):
    pos = jnp.arange(n)
    half = RET_HEAD // 2
    inv = ROPE_BASE ** (-jnp.arange(0, half, 2, dtype=F32) / half)

    def ang(p):
        a = p.astype(F32)[:, None] * inv
        return jnp.concatenate([a, a], axis=-1)

    a = jnp.concatenate([ang(pos // GRID_W), ang(pos % GRID_W)], axis=-1)
    sign = jnp.tile(jnp.concatenate([-jnp.ones((half // 2,), F32), jnp.ones((half // 2,), F32)]), 2)
    cos = jnp.concatenate([jnp.ones((CTX_LEN, RET_HEAD), F32), jnp.cos(a)], axis=0)
    sin = jnp.concatenate([jnp.zeros((CTX_LEN, RET_HEAD), F32), jnp.sin(a) * sign], axis=0)
    return cos, sin


def _pad_rows(w, rows, offset):
    out = jnp.zeros((rows,) + w.shape[1:], w.dtype)
    return lax.dynamic_update_slice_in_dim(out, w, offset, axis=0)


def kernel(x, c, ctx, c_ctx, ada_w, ada_b, norm1_g, norm2_g, w_in, shift_mu, rwkv_w0, rwkv_w2, rwkv_a0, rwkv_a2,
           rwkv_g2, rwkv_k_k, rwkv_k_a, rwkv_r_k, rwkv_lnx_w, rwkv_lnx_b, ret_log2_decay, ret_gn_w, w_out,
           router_w, router_b, moe_w1, moe_w2, final_g):
    b, n, d = x.shape
    assert b == 1 and d == D_MODEL and ctx.shape[1] == CTX_LEN and n % TOKEN_TILE == 0
    depth = ada_w.shape[0]
    w = RWKV_WIDTH

    xa = jnp.concatenate([ctx[0], x[0]], axis=0)
    cct = jnp.stack([c[0], c_ctx], axis=1)
    mod = _ada_mod(cct, ada_w, ada_b).reshape(depth, 2, 6, d)
    cos, sin = _rope_tables(n)
    rwt = router_w.T
    w1b = moe_w1.astype(BF16)
    w2b = moe_w2.astype(BF16)

    out = None
    for l in range(depth):
        last = l == depth - 1
        sh1, sc1, gt1, sh2, sc2, gt2 = (mod[l, :, i, :] for i in range(6))
        h = _norm_mod(xa, norm1_g[l], sc1, sh1)
        w_pad = jnp.concatenate(
            [w_in[l, :, :RWKV_IN], jnp.zeros((d, RWKV_PAD - RWKV_IN), F32), w_in[l, :, RWKV_IN:]], axis=1)
        p = _matmul(h, w_pad.astype(BF16), BF16)

        mu = jnp.pad(shift_mu[l], ((0, 0), (0, RWKV_PAD - RWKV_IN)))
        w2p = jnp.stack([_pad_rows(rwkv_w2[l, 0], LANE, 0), _pad_rows(rwkv_w2[l, 1], LANE, DECAY_LORA)])
        a2p = jnp.stack([_pad_rows(rwkv_a2[l, 0], LANE, 0), _pad_rows(rwkv_a2[l, 1], LANE, ICLR_LORA)])
        g2p = _pad_rows(rwkv_g2[l], 2 * LANE, 0)
        r, v, kk, g, bonus, lw, kd, bb = _rwkv_prep(
            p, mu, w2p.astype(BF16), a2p.astype(BF16), g2p.astype(BF16), rwkv_w0[l], rwkv_a0[l],
            rwkv_k_k[l].reshape(1, w), rwkv_k_a[l].reshape(1, w), rwkv_r_k[l].reshape(1, w))
        ya = _rwkv_scan(r, v, kk, lw, kd, bb)

        l2d = jnp.broadcast_to(ret_log2_decay[l].astype(F32)[:, :, None], (2, RET_HEADS, LANE))
        yb = _retention(p, cos, sin, l2d, ret_gn_w[l])

        xa, h2, scores_t = _mix_out(ya, bonus, g, yb, xa, w_out[l].astype(BF16), rwkv_lnx_w[l], rwkv_lnx_b[l],
                                    gt1, norm2_g[l], sc2, sh2, rwt)
        expert, gate = _route(scores_t, router_b)
        block_expert, row_src, n_valid, n_used = _moe_plan(expert)
        o2 = _moe_experts(h2, block_expert, row_src, n_valid, n_used, w1b[l], w2b[l])
        res = _combine(o2, gate.T, xa, gt2, final_g, last)
        if last:
            out = res
        else:
            xa = res
    return out[None]
```

```python
import functools
import math

import jax
import jax.numpy as jnp
from jax import lax
from jax.experimental import pallas as pl
from jax.experimental.pallas import tpu as pltpu

F32 = jnp.float32
BF16 = jnp.bfloat16
HIGHEST = lax.Precision.HIGHEST

D_MODEL = 2048
CTX_LEN = 256
GRID_W = 64
NORM_EPS = 1e-6

RWKV_WIDTH = 1024
RWKV_HEAD = 64
DECAY_LORA = 64
ICLR_LORA = 64
GATE_LORA = 160
LNX_EPS = 64e-5
RWKV_IN = 3 * RWKV_WIDTH + 2 * DECAY_LORA + 2 * ICLR_LORA + GATE_LORA
LORA_PAD = 512
RWKV_PAD = 3 * RWKV_WIDTH + LORA_PAD

RET_WIDTH = 1024
RET_HEAD = 128
RET_HEADS = 8
RET_CHUNK = 128
ROPE_BASE = 10000.0
P_PAD = RWKV_PAD + 4 * RET_WIDTH

N_EXPERTS = 32
N_GROUPS = 4
EXPERTS_PER_GROUP = 8
TOP_K = 2
D_EXPERT = 1024
MOE_BLOCK = 128

LANE = 128
TOKEN_TILE = 256
RWKV_CHUNK = 64
VMEM_LIMIT = 48 * 1024 * 1024
MOE_VMEM_LIMIT = 56 * 1024 * 1024


def _cparams(sem):
    return pltpu.CompilerParams(dimension_semantics=sem, vmem_limit_bytes=VMEM_LIMIT)


def _dot(a, b, precision=None):
    return jnp.dot(a, b, preferred_element_type=F32, precision=precision)


def _dot_nt(a, b, precision=None):
    return lax.dot_general(a, b, (((1,), (1,)), ((), ())), preferred_element_type=F32, precision=precision)


def _dot_tn(a, b, precision=None):
    return lax.dot_general(a, b, (((0,), (0,)), ((), ())), preferred_element_type=F32, precision=precision)


def _split_bf16(x):
    hi = x.astype(BF16)
    lo = (x - hi.astype(F32)).astype(BF16)
    return hi, lo


def _split3_bf16(x):
    hi = x.astype(BF16)
    rest = x - hi.astype(F32)
    mid = rest.astype(BF16)
    return hi, mid, (rest - mid.astype(F32)).astype(BF16)


def _head_sum(x, head):
    ri = lax.broadcasted_iota(jnp.int32, (LANE, LANE), 0) // head
    ci = lax.broadcasted_iota(jnp.int32, (LANE, LANE), 1) // head
    bd = (ri == ci).astype(BF16)
    outs = []
    for j in range(x.shape[1] // LANE):
        hi, lo = _split_bf16(x[:, j * LANE:(j + 1) * LANE])
        outs.append(_dot(hi, bd) + _dot(lo, bd))
    return outs[0] if len(outs) == 1 else jnp.concatenate(outs, axis=1)


def _ada_kernel(cct_ref, w_ref, b_ref, o_ref):
    k = pl.program_id(2)

    @pl.when(k == 0)
    def _():
        o_ref[0] = jnp.broadcast_to(b_ref[0], o_ref.shape[1:])

    s = cct_ref[...]
    s = s * jax.nn.sigmoid(s)
    w = w_ref[0]
    acc0 = jnp.sum(s[:, 0:1] * w, axis=0, keepdims=True)
    acc1 = jnp.sum(s[:, 1:2] * w, axis=0, keepdims=True)
    o_ref[0] += jnp.concatenate([acc0, acc1], axis=0)


def _ada_mod(cct, ada_w, ada_b):
    depth, d, n6 = ada_w.shape
    tk, tn = 512, 2048
    return pl.pallas_call(
        _ada_kernel,
        grid=(depth, n6 // tn, d // tk),
        in_specs=[
            pl.BlockSpec((tk, 2), lambda l, j, k: (k, 0)),
            pl.BlockSpec((1, tk, tn), lambda l, j, k: (l, k, j)),
            pl.BlockSpec((1, 1, tn), lambda l, j, k: (l, 0, j)),
        ],
        out_specs=pl.BlockSpec((1, 2, tn), lambda l, j, k: (l, 0, j)),
        out_shape=jax.ShapeDtypeStruct((depth, 2, n6), F32),
        compiler_params=_cparams(("parallel", "parallel", "arbitrary")),
        name="ada_mod",
    )(cct, ada_w, ada_b.reshape(depth, 1, n6))


def _rms(x):
    return x * lax.rsqrt(jnp.mean(x * x, axis=-1, keepdims=True) + NORM_EPS)


def _norm_mod_kernel(x_ref, g_ref, sc_ref, sh_ref, o_ref):
    is_ctx = pl.program_id(0) == 0
    y = _rms(x_ref[...]) * g_ref[...]
    sc = jnp.where(is_ctx, sc_ref[1:2, :], sc_ref[0:1, :])
    sh = jnp.where(is_ctx, sh_ref[1:2, :], sh_ref[0:1, :])
    o_ref[...] = (y * (1.0 + sc) + sh).astype(o_ref.dtype)


def _norm_mod(xa, g, sc, sh):
    t, d = xa.shape
    row = pl.BlockSpec((1, d), lambda i: (0, 0))
    two = pl.BlockSpec((2, d), lambda i: (0, 0))
    return pl.pallas_call(
        _norm_mod_kernel,
        grid=(t // TOKEN_TILE,),
        in_specs=[pl.BlockSpec((TOKEN_TILE, d), lambda i: (i, 0)), row, two, two],
        out_specs=pl.BlockSpec((TOKEN_TILE, d), lambda i: (i, 0)),
        out_shape=jax.ShapeDtypeStruct((t, d), BF16),
        compiler_params=_cparams(("parallel",)),
        name="norm_mod",
    )(xa, g.reshape(1, d), sc, sh)


def _matmul_kernel(a_ref, b_ref, o_ref):
    o_ref[...] = _dot(a_ref[...], b_ref[...]).astype(o_ref.dtype)


def _row_tile(t, cap):
    tm = TOKEN_TILE
    for cand in range(TOKEN_TILE, cap + 1, TOKEN_TILE):
        if t % cand == 0:
            tm = cand
    return tm


def _matmul(a, b, out_dtype):
    m, k = a.shape
    n = b.shape[1]
    tm, tn = _row_tile(m, 1280), 512
    return pl.pallas_call(
        _matmul_kernel,
        grid=(m // tm, n // tn),
        in_specs=[pl.BlockSpec((tm, k), lambda i, j: (i, 0)), pl.BlockSpec((k, tn), lambda i, j: (0, j))],
        out_specs=pl.BlockSpec((tm, tn), lambda i, j: (i, j)),
        out_shape=jax.ShapeDtypeStruct((m, n), out_dtype),
        compiler_params=_cparams(("parallel", "arbitrary")),
        name="in_proj",
    )(a, b)


def _rwkv_prep_kernel(pm_ref, pp_ref, pn_ref, mu_ref, w2_ref, a2_ref, g2_ref, w0_ref, a0_ref, kk_ref, ka_ref,
                      rk_ref, r_out, v_out, kk_out, g_out, bonus_out, lw_out, kd_out, b_out):
    i = pl.program_id(0)
    nt = pl.num_programs(0)
    is_ctx = i == 0
    tm = TOKEN_TILE
    row = lax.broadcasted_iota(jnp.int32, (tm, 1), 0)
    in_row = row & (GRID_W - 1)
    edge_lo = jnp.where(is_ctx, row, in_row) == 0
    edge_hi = jnp.where(is_ctx, row - (tm - 1), in_row - (GRID_W - 1)) == 0
    no_up = jnp.logical_or(is_ctx, i == 1)
    no_down = jnp.logical_or(is_ctx, i == nt - 1)
    x_gate = jnp.where(is_ctx, 0.0, 1.0)

    def shifted(c0, c1):
        p = pm_ref[:, c0:c1].astype(F32)
        mu = mu_ref[:, c0:c1]
        prev = jnp.where(edge_lo, 0.0, pltpu.roll(p, 1, axis=0))
        nxt = jnp.where(edge_hi, 0.0, pltpu.roll(p, tm - 1, axis=0))
        out = p + mu[0:1] * (prev - p) + mu[1:2] * (nxt - p)
        above = jnp.where(no_up, 0.0, pp_ref[:, c0:c1].astype(F32))
        below = jnp.where(no_down, 0.0, pn_ref[:, c0:c1].astype(F32))
        up = jnp.concatenate([above, p[:tm - GRID_W]], axis=0)
        down = jnp.concatenate([p[GRID_W:], below], axis=0)
        return out + x_gate * (mu[2:3] * (up - p) + mu[3:4] * (down - p))

    w = RWKV_WIDTH
    ul = shifted(3 * w, 3 * w + LORA_PAD)
    twd = jnp.tanh(ul[:, 0:LANE]).astype(BF16)
    ad = ul[:, LANE:2 * LANE].astype(BF16)
    sg = jax.nn.sigmoid(ul[:, 2 * LANE:4 * LANE]).astype(BF16)
    g_out[...] = _dot(sg, g2_ref[...]).astype(g_out.dtype)

    r = shifted(0, w)
    k = shifted(w, 2 * w)
    v = shifted(2 * w, 3 * w)
    r_out[...] = r.astype(r_out.dtype)
    v_out[...] = v.astype(v_out.dtype)
    kkr = k * kk_ref[...]
    nrm = jnp.sqrt(_head_sum(kkr * kkr, RWKV_HEAD))
    kk = kkr / jnp.maximum(nrm, 1e-12)
    kk_out[...] = kk.astype(kk_out.dtype)
    ksum = jnp.zeros_like(k)
    for d in range(2):
        z = w0_ref[d:d + 1, :] + _dot(twd, w2_ref[d])
        lw_out[d] = -jax.nn.sigmoid(z) * math.exp(-0.5)
        a = jax.nn.sigmoid(a0_ref[d:d + 1, :] + _dot(ad, a2_ref[d]))
        kd = k * (1.0 + (a - 1.0) * ka_ref[...])
        kd_out[d] = kd.astype(kd_out.dtype)
        b_out[d] = (a * kk).astype(b_out.dtype)
        ksum = ksum + kd
    bonus = _head_sum(r * ksum * rk_ref[...], RWKV_HEAD) * v
    bonus_out[...] = bonus.astype(bonus_out.dtype)


def _rwkv_prep(p, mu, w2p, a2p, g2p, w0, a0, k_k, k_a, r_k):
    t = p.shape[0]
    nt = t // TOKEN_TILE
    per_tile = TOKEN_TILE // GRID_W
    n_rows = t // GRID_W
    w = RWKV_WIDTH
    full = lambda shape: pl.BlockSpec(shape, lambda i: (0,) * len(shape))
    tok = pl.BlockSpec((TOKEN_TILE, w), lambda i: (i, 0))
    tok2 = pl.BlockSpec((2, TOKEN_TILE, w), lambda i: (0, i, 0))
    one = jax.ShapeDtypeStruct((t, w), BF16)
    return pl.pallas_call(
        _rwkv_prep_kernel,
        grid=(nt,),
        in_specs=[
            pl.BlockSpec((TOKEN_TILE, RWKV_PAD), lambda i: (i, 0)),
            pl.BlockSpec((GRID_W, RWKV_PAD), lambda i: (jnp.maximum(i * per_tile - 1, 0), 0)),
            pl.BlockSpec((GRID_W, RWKV_PAD), lambda i: (jnp.minimum((i + 1) * per_tile, n_rows - 1), 0)),
            full((4, RWKV_PAD)), full((2, LANE, w)), full((2, LANE, w)), full((2 * LANE, w)),
            full((2, w)), full((2, w)), full((1, w)), full((1, w)), full((1, w)),
        ],
        out_specs=[tok, tok, tok, tok, tok, tok2, tok2, tok2],
        out_shape=[one, one, one, one, one,
                   jax.ShapeDtypeStruct((2, t, w), F32),
                   jax.ShapeDtypeStruct((2, t, w), BF16),
                   jax.ShapeDtypeStruct((2, t, w), BF16)],
        compiler_params=_cparams(("parallel",)),
        name="rwkv_prep",
    )(p, p, p, mu, w2p, a2p, g2p, w0, a0, k_k, k_a, r_k)


RWKV_TILE_PAIRS = 4


def _stack_heads(x):
    lane = lax.broadcasted_iota(jnp.int32, (1, LANE), 1)
    zero = jnp.zeros_like(x)
    return jnp.concatenate([jnp.where(lane < RWKV_HEAD, x, zero), jnp.where(lane >= RWKV_HEAD, x, zero)], axis=0)


def _rwkv_scan_kernel(r_ref, v_ref, kk_ref, lw_ref, kd_ref, b_ref, y_ref, m_ref, lq_s, yl_s, ml_s, el_s):
    d = pl.program_id(0)
    s = pl.program_id(2)
    sign = 1 - 2 * d
    c = RWKV_CHUNK
    tm = TOKEN_TILE
    n_chunks = tm // c
    n_pairs = r_ref.shape[1] // LANE

    @pl.when(s == 0)
    def _():
        m_ref[...] = jnp.zeros_like(m_ref)

    ri = lax.broadcasted_iota(jnp.int32, (tm, tm), 0)
    ci = lax.broadcasted_iota(jnp.int32, (tm, tm), 1)
    same_chunk = (ri // c) == (ci // c)
    along = ((ci - ri) * sign) <= 0
    tri = jnp.where(jnp.logical_and(same_chunk, along), 1.0, 0.0).astype(BF16)
    lw = lw_ref[...]
    lw3 = _split3_bf16(lw)
    sums = sum(_dot(jnp.concatenate([tri, jnp.where(same_chunk, 1.0, 0.0).astype(BF16)], axis=0), piece)
               for piece in lw3)
    lc = sums[:tm]
    lend = sums[tm:]
    e_neg = jnp.exp(-lc)
    e_end = jnp.exp(lend - lc)
    kd = kd_ref[...].astype(F32)
    b = b_ref[...].astype(F32)
    rl_all = r_ref[...].astype(F32) * jnp.exp(lc)
    kke_all = (kk_ref[...].astype(F32) * jnp.exp(lc - lw)).astype(BF16)
    kinv_all = (kd * e_neg).astype(BF16)
    binv_all = (b * e_neg).astype(BF16)
    kdec_all = (kd * e_end).astype(BF16)
    bdec_all = (b * e_end).astype(BF16)

    trow = lax.broadcasted_iota(jnp.int32, (c, LANE), 0)
    tcol = lax.broadcasted_iota(jnp.int32, (c, LANE), 1) & (RWKV_HEAD - 1)
    rel = (tcol - trow) * sign
    incl = rel <= 0
    strict = rel < 0
    eye = jnp.where(rel == 0, 1.0, 0.0)
    hi = lax.broadcasted_iota(jnp.int32, (LANE, LANE), 0) < RWKV_HEAD
    hj = lax.broadcasted_iota(jnp.int32, (LANE, LANE), 1) < RWKV_HEAD
    same_head = hi == hj

    units = [(cc, pr) for cc in range(n_chunks) for pr in range(n_pairs)]

    def sl(x, u):
        cc, pr = u
        return x[cc * c:(cc + 1) * c, pr * LANE:(pr + 1) * LANE]

    rl = [sl(rl_all, u) for u in units]
    kke = [sl(kke_all, u) for u in units]
    vb = [v_ref[u[0] * c:(u[0] + 1) * c, u[1] * LANE:(u[1] + 1) * LANE] for u in units]
    vst = [_stack_heads(x) for x in vb]
    n_units = len(units)
    aa = [_dot_nt(jnp.concatenate([kke[i], rl[i].astype(BF16)], axis=0),
                  jnp.concatenate([_stack_heads(sl(kinv_all, u)), _stack_heads(sl(binv_all, u))], axis=0))
          for i, u in enumerate(units)]
    a_kr = [jnp.concatenate([jnp.where(strict, x[:c, :LANE], 0.0), jnp.where(incl, x[c:, :LANE], 0.0)],
                            axis=0).astype(BF16) for x in aa]
    a_rb = [jnp.where(incl, x[c:, LANE:], 0.0).astype(BF16) for x in aa]
    pw = [jnp.where(strict, -x[:c, LANE:], 0.0) for x in aa]
    tinv = [eye + x for x in pw]
    for _ in range(5):
        pwb = [x.astype(BF16) for x in pw]
        pw = [_dot(x, _stack_heads(x)) for x in pwb]
        tinv = [t + _dot(t.astype(BF16), _stack_heads(x.astype(BF16))) for t, x in zip(tinv, pw)]
    tb = [t.astype(BF16) for t in tinv]
    avr = [_dot(a, x) for a, x in zip(a_kr, vst)]
    w12 = [_dot(tb[i], jnp.concatenate([_stack_heads(kke[i]), _stack_heads(avr[i][:c].astype(BF16))], axis=1))
           .astype(BF16) for i in range(n_units)]
    corr = [_dot(a_rb[i], jnp.concatenate([_stack_heads(w12[i][:, :LANE]), _stack_heads(w12[i][:, LANE:])], axis=1))
            for i in range(n_units)]
    ones = jnp.ones((3 * c, LANE), BF16)
    for i, u in enumerate(units):
        cc, pr = u
        bw = _dot_tn(sl(bdec_all, u), w12[i])
        lq = jnp.concatenate([rl[i] - corr[i][:, :LANE], jnp.where(same_head, -bw[:, :LANE], 0.0)], axis=0)
        l_hi, l_lo = _split_bf16(lq)
        lq_s[cc, pr] = jnp.concatenate([l_hi, l_lo], axis=1)
        yl_s[cc, pr] = avr[i][c:] - corr[i][:, LANE:]
        ml_s[cc, pr] = jnp.where(same_head, _dot_tn(sl(kdec_all, u), vb[i]) - bw[:, LANE:], 0.0)
        el_s[cc, pr] = jnp.exp(_dot_tn(jnp.concatenate([sl(piece, u) for piece in lw3], axis=0), ones))

    m = [m_ref[pr] for pr in range(n_pairs)]
    for step in range(n_chunks):
        cc = jnp.where(d == 1, n_chunks - 1 - step, step)
        rows = pl.ds(pl.multiple_of(cc * c, c), c)
        for pr in range(n_pairs):
            m_hi, m_lo = _split_bf16(m[pr])
            lq = lq_s[cc, pr]
            res = _dot(lq, jnp.concatenate([m_hi, m_hi], axis=0)) + _dot(lq[:, :LANE], m_lo)
            y_ref[rows, pr * LANE:(pr + 1) * LANE] = (res[:c] + yl_s[cc, pr]).astype(y_ref.dtype)
            m[pr] = el_s[cc, pr] * m[pr] + res[c:] + ml_s[cc, pr]
    for pr in range(n_pairs):
        m_ref[pr] = m[pr]


def _rwkv_scan(r, v, kk, lw, kd, b):
    t, w = r.shape
    nt = t // TOKEN_TILE
    wb = RWKV_TILE_PAIRS * LANE
    n_chunks = TOKEN_TILE // RWKV_CHUNK
    n_pairs = RWKV_TILE_PAIRS

    def tile(d, s):
        return jnp.where(jnp.logical_and(d == 1, s > 0), nt - s, s)

    shared = pl.BlockSpec((TOKEN_TILE, wb), lambda d, j, s: (tile(d, s), j))
    per_dir = pl.BlockSpec((None, TOKEN_TILE, wb), lambda d, j, s: (d, tile(d, s), j))
    unit = lambda rows: pltpu.VMEM((n_chunks, n_pairs, rows, LANE), F32)
    return pl.pallas_call(
        _rwkv_scan_kernel,
        grid=(2, w // wb, nt),
        in_specs=[shared, shared, shared, per_dir, per_dir, per_dir],
        out_specs=per_dir,
        out_shape=jax.ShapeDtypeStruct((2, t, w), BF16),
        scratch_shapes=[pltpu.VMEM((n_pairs, LANE, LANE), F32),
                        pltpu.VMEM((n_chunks, n_pairs, RWKV_CHUNK + LANE, 2 * LANE), BF16),
                        unit(RWKV_CHUNK), unit(LANE), unit(LANE)],
        compiler_params=_cparams(("parallel", "parallel", "arbitrary")),
        name="rwkv_scan",
    )(r, v, kk, lw, kd, b)


def _rotate(t, cos, sin_signed):
    lane = lax.broadcasted_iota(jnp.int32, (1, LANE), 1)
    first = (lane & 63) < 32
    swapped = jnp.where(first, pltpu.roll(t, LANE - 32, axis=1), pltpu.roll(t, 32, axis=1))
    return t * cos + swapped * sin_signed


RET_TILE_HEADS = 2


def _ret_qk(q_ref, k_ref, cos_ref, sin_ref, rows, lanes):
    cos = cos_ref[rows, :]
    sin = sin_ref[rows, :]
    q = _rotate(q_ref[rows, lanes].astype(F32), cos, sin)
    k = _rotate(k_ref[rows, lanes].astype(F32) * (RET_HEAD ** -0.5), cos, sin)
    return q, k


def _log_gamma(l2d_ref, hh):
    lg = jnp.log1p(-jnp.exp2(-l2d_ref[hh:hh + 1, :]))
    return jnp.broadcast_to(lg, (RET_CHUNK, LANE))


def _ret_units(reverse):
    chunks = range(TOKEN_TILE // RET_CHUNK)
    for hh in range(RET_TILE_HEADS):
        for cc in (reversed(chunks) if reverse else chunks):
            yield hh, slice(cc * RET_CHUNK, (cc + 1) * RET_CHUNK), slice(hh * LANE, (hh + 1) * LANE)


def _ret_fwd_kernel(q_ref, k_ref, v_ref, cos_ref, sin_ref, l2d_ref, y_ref, s_ref):
    @pl.when(pl.program_id(1) == 0)
    def _():
        s_ref[...] = jnp.zeros_like(s_ref)

    c = RET_CHUNK
    idx = lax.broadcasted_iota(jnp.int32, (c, LANE), 0).astype(F32)
    for hh, rows, lanes in _ret_units(False):
        q, k = _ret_qk(q_ref, k_ref, cos_ref, sin_ref, rows, lanes)
        lg = _log_gamma(l2d_ref, hh)
        s = s_ref[hh]
        y_ref[rows, lanes] = _dot((q * jnp.exp((idx + 1.0) * lg)).astype(BF16), s.astype(BF16)).astype(y_ref.dtype)
        kz = (k * jnp.exp((c - 1.0 - idx) * lg)).astype(BF16)
        s_ref[hh] = s * jnp.exp(c * lg) + _dot_tn(kz, v_ref[rows, lanes])


def _ret_bwd_kernel(q_ref, k_ref, v_ref, g_ref, y1_ref, cos_ref, sin_ref, l2f_ref, l2b_ref, gn_ref, o_ref, s_ref):
    @pl.when(pl.program_id(1) == 0)
    def _():
        s_ref[...] = jnp.zeros_like(s_ref)

    c = RET_CHUNK
    rowi = lax.broadcasted_iota(jnp.int32, (c, LANE), 0)
    coli = lax.broadcasted_iota(jnp.int32, (c, LANE), 1)
    diff = (rowi - coli).astype(F32)
    idx = rowi.astype(F32)
    for hh, rows, lanes in _ret_units(True):
        q, k = _ret_qk(q_ref, k_ref, cos_ref, sin_ref, rows, lanes)
        v = v_ref[rows, lanes]
        lgf = _log_gamma(l2f_ref, hh)
        lgb = _log_gamma(l2b_ref, hh)
        dmat = jnp.where(rowi >= coli, jnp.exp(diff * lgf), jnp.exp(-diff * lgb))
        scores = (_dot_nt(q.astype(BF16), k.astype(BF16)) * dmat).astype(BF16)
        s = s_ref[hh]
        y = (y1_ref[rows, lanes].astype(F32) + _dot(scores, v)
             + _dot((q * jnp.exp((c - idx) * lgb)).astype(BF16), s.astype(BF16)))
        s_ref[hh] = s * jnp.exp(c * lgb) + _dot_tn((k * jnp.exp(idx * lgb)).astype(BF16), v)
        yn = y * lax.rsqrt(jnp.mean(y * y, axis=-1, keepdims=True) + NORM_EPS)
        g = g_ref[rows, lanes].astype(F32)
        o_ref[rows, lanes] = (yn * gn_ref[:, lanes] * (g * jax.nn.sigmoid(g))).astype(o_ref.dtype)


def _retention(p, cos, sin, l2d, gn_w):
    t = p.shape[0]
    nt = t // TOKEN_TILE
    wb = RET_TILE_HEADS * LANE
    n_col = RET_WIDTH // wb
    base = RWKV_PAD // wb
    l2d = l2d.reshape(2, n_col, RET_TILE_HEADS, LANE)

    def rtile(s):
        return jnp.where(s > 0, nt - s, s)

    blk = (TOKEN_TILE, wb)
    rope = (TOKEN_TILE, LANE)
    dec = lambda d: pl.BlockSpec((None, None, RET_TILE_HEADS, LANE), lambda j, s: (d, j, 0, 0))
    col = lambda part: pl.BlockSpec(blk, lambda j, s: (s, base + part * n_col + j))
    rcol = lambda part: pl.BlockSpec(blk, lambda j, s: (rtile(s), base + part * n_col + j))
    scratch = [pltpu.VMEM((RET_TILE_HEADS, RET_HEAD, RET_HEAD), F32)]
    y1 = pl.pallas_call(
        _ret_fwd_kernel,
        grid=(n_col, nt),
        in_specs=[col(0), col(1), col(2), pl.BlockSpec(rope, lambda j, s: (s, 0)),
                  pl.BlockSpec(rope, lambda j, s: (s, 0)), dec(0)],
        out_specs=pl.BlockSpec(blk, lambda j, s: (s, j)),
        out_shape=jax.ShapeDtypeStruct((t, RET_WIDTH), BF16),
        scratch_shapes=scratch,
        compiler_params=_cparams(("parallel", "arbitrary")),
        name="ret_fwd",
    )(p, p, p, cos, sin, l2d)
    return pl.pallas_call(
        _ret_bwd_kernel,
        grid=(n_col, nt),
        in_specs=[rcol(0), rcol(1), rcol(2), rcol(3), pl.BlockSpec(blk, lambda j, s: (rtile(s), j)),
                  pl.BlockSpec(rope, lambda j, s: (rtile(s), 0)), pl.BlockSpec(rope, lambda j, s: (rtile(s), 0)),
                  dec(0), dec(1), pl.BlockSpec((1, wb), lambda j, s: (0, j))],
        out_specs=pl.BlockSpec(blk, lambda j, s: (rtile(s), j)),
        out_shape=jax.ShapeDtypeStruct((t, RET_WIDTH), BF16),
        scratch_shapes=scratch,
        compiler_params=_cparams(("parallel", "arbitrary")),
        name="ret_bwd",
    )(p, p, p, p, y1, cos, sin, l2d, l2d, gn_w.reshape(1, RET_WIDTH))


def _mix_out_kernel(ya_ref, bonus_ref, g_ref, yb_ref, x_ref, wo_ref, lnw_ref, lnb_ref, gt_ref, n2_ref, sc_ref,
                    sh_ref, rw_ref, x_out, h_out, s_out):
    is_ctx = pl.program_id(0) == 0
    y = ya_ref[0].astype(F32) + ya_ref[1].astype(F32)
    inv_n = 1.0 / RWKV_HEAD
    mu = _head_sum(y, RWKV_HEAD) * inv_n
    yc = y - mu
    var = _head_sum(yc * yc, RWKV_HEAD) * inv_n
    yn = yc * lax.rsqrt(var + LNX_EPS)
    ya = (yn * lnw_ref[...] + lnb_ref[...] + bonus_ref[...].astype(F32)) * g_ref[...].astype(F32)
    z = _dot(ya.astype(BF16), wo_ref[0:RWKV_WIDTH, :]) + _dot(yb_ref[...], wo_ref[RWKV_WIDTH:, :])
    pick = lambda ref: jnp.where(is_ctx, ref[1:2, :], ref[0:1, :])
    xn = x_ref[...] + pick(gt_ref) * z
    x_out[...] = xn
    h = _rms(xn) * n2_ref[...] * (1.0 + pick(sc_ref)) + pick(sh_ref)
    h_out[...] = h
    s_out[...] = jax.nn.sigmoid(_dot_nt(rw_ref[...], h, precision=HIGHEST))


def _mix_out(ya, bonus, g, yb, xa, wo, lnw, lnb, gt, n2, sc, sh, rwt):
    t, d = xa.shape
    w = RWKV_WIDTH
    tm = TOKEN_TILE
    row = lambda n: pl.BlockSpec((1, n), lambda i: (0, 0))
    two = pl.BlockSpec((2, d), lambda i: (0, 0))
    tok = lambda n: pl.BlockSpec((tm, n), lambda i: (i, 0))
    return pl.pallas_call(
        _mix_out_kernel,
        grid=(t // tm,),
        in_specs=[pl.BlockSpec((2, tm, w), lambda i: (0, i, 0)), tok(w), tok(w), tok(w), tok(d),
                  pl.BlockSpec((d, d), lambda i: (0, 0)), row(w), row(w), two, row(d), two, two,
                  pl.BlockSpec((N_EXPERTS, d), lambda i: (0, 0))],
        out_specs=[tok(d), tok(d), pl.BlockSpec((N_EXPERTS, tm), lambda i: (0, i))],
        out_shape=[jax.ShapeDtypeStruct((t, d), F32), jax.ShapeDtypeStruct((t, d), F32),
                   jax.ShapeDtypeStruct((N_EXPERTS, t), F32)],
        compiler_params=_cparams(("parallel",)),
        name="mix_out",
    )(ya, bonus, g, yb, xa, wo, lnw.reshape(1, w), lnb.reshape(1, w), gt, n2.reshape(1, d), sc, sh, rwt)


def _first_argmax(x, idx, n):
    m = jnp.max(x, axis=0, keepdims=True)
    return m, jnp.min(jnp.where(x == m, idx, n), axis=0, keepdims=True)


def _route_kernel(s_ref, rb_ref, e_out, w_out):
    epg = EXPERTS_PER_GROUP
    tn = s_ref.shape[1]
    idx = lax.broadcasted_iota(jnp.int32, (epg, tn), 0)
    neg = -jnp.inf
    best = None
    for gi in range(N_GROUPS):
        sc = s_ref[gi * epg:(gi + 1) * epg, :]
        bi = sc + rb_ref[gi * epg:(gi + 1) * epg, :]
        m1, i1 = _first_argmax(bi, idx, epg)
        m2 = jnp.max(jnp.where(idx == i1, neg, bi), axis=0, keepdims=True)
        gs = m1 + m2
        if best is None:
            best, g_sel, sc_sel, bi_sel = gs, jnp.zeros((1, tn), jnp.int32), sc, bi
        else:
            better = gs > best
            best = jnp.where(better, gs, best)
            g_sel = jnp.where(better, gi, g_sel)
            sc_sel = jnp.where(better, sc, sc_sel)
            bi_sel = jnp.where(better, bi, bi_sel)
    _, l1 = _first_argmax(bi_sel, idx, epg)
    _, l2 = _first_argmax(jnp.where(idx == l1, neg, bi_sel), idx, epg)
    w1 = jnp.sum(jnp.where(idx == l1, sc_sel, 0.0), axis=0, keepdims=True)
    w2 = jnp.sum(jnp.where(idx == l2, sc_sel, 0.0), axis=0, keepdims=True)
    tot = w1 + w2
    e_out[...] = jnp.concatenate([g_sel * epg + l1, g_sel * epg + l2], axis=0)
    w_out[...] = jnp.concatenate([w1 / tot, w2 / tot], axis=0)


def _route(scores_t, router_b):
    t = scores_t.shape[1]
    tn = _row_tile(t, 1280)
    return pl.pallas_call(
        _route_kernel,
        grid=(t // tn,),
        in_specs=[pl.BlockSpec((N_EXPERTS, tn), lambda i: (0, i)), pl.BlockSpec((N_EXPERTS, 1), lambda i: (0, 0))],
        out_specs=[pl.BlockSpec((TOP_K, tn), lambda i: (0, i)), pl.BlockSpec((TOP_K, tn), lambda i: (0, i))],
        out_shape=[jax.ShapeDtypeStruct((TOP_K, t), jnp.int32), jax.ShapeDtypeStruct((TOP_K, t), F32)],
        compiler_params=_cparams(("parallel",)),
        name="route",
    )(scores_t, router_b.reshape(N_EXPERTS, 1))


def _moe_kernel(be_ref, src_ref, nused_ref, h_hbm, w1_ref, w2_ref, o_hbm, xbuf, ybuf, w1b, w2b, gsem, ssem):
    blk = pl.program_id(0)
    n_used = nused_ref[0]
    d = xbuf.shape[2]
    last_tok = h_hbm.shape[0] - 1
    slot = blk & 1
    assert TOP_K == 2

    def issue_gather(b, sl):
        def body(r, carry):
            tok = jnp.minimum(jnp.right_shift(src_ref[b * MOE_BLOCK + r], 1), last_tok)
            pltpu.make_async_copy(h_hbm.at[pl.ds(tok, 1)], xbuf.at[sl, pl.ds(r, 1)], gsem.at[sl]).start()
            return carry

        lax.fori_loop(0, MOE_BLOCK, body, 0, unroll=8)

    def wait_scatter():
        pltpu.make_async_copy(ybuf, o_hbm.at[pl.ds(0, MOE_BLOCK), pl.ds(0, d)], ssem).wait()

    @pl.when(blk < n_used)
    def _():
        @pl.when(blk == 0)
        def _():
            ybuf[...] = jnp.zeros_like(ybuf)
            for k in range(TOP_K):
                spare = pltpu.make_async_copy(ybuf, o_hbm.at[pl.ds(last_tok + 1, MOE_BLOCK), pl.ds(k * d, d)], ssem)
                spare.start()
                spare.wait()
            issue_gather(blk, slot)

        @pl.when(blk + 1 < n_used)
        def _():
            issue_gather(blk + 1, 1 - slot)

        @pl.when(jnp.logical_or(blk == 0, be_ref[blk] != be_ref[jnp.maximum(blk - 1, 0)]))
        def _():
            w1b[...] = w1_ref[...].astype(BF16)
            w2b[...] = w2_ref[...].astype(BF16)

        pltpu.make_async_copy(h_hbm.at[pl.ds(0, MOE_BLOCK)], xbuf.at[slot], gsem.at[slot]).wait()
        gu = _dot(xbuf[slot].astype(BF16), w1b[...])
        gate = gu[:, :D_EXPERT]
        act = (gate * jax.nn.sigmoid(gate) * gu[:, D_EXPERT:]).astype(BF16)

        @pl.when(blk > 0)
        def _():
            wait_scatter()

        ybuf[...] = _dot(act, w2b[...])

        def start_scatter(r, carry):
            src = src_ref[blk * MOE_BLOCK + r]
            lane0 = pl.multiple_of((src & 1) * d, d)
            pltpu.make_async_copy(ybuf.at[pl.ds(r, 1)],
                                  o_hbm.at[pl.ds(jnp.right_shift(src, 1), 1), pl.ds(lane0, d)], ssem).start()
            return carry

        lax.fori_loop(0, MOE_BLOCK, start_scatter, 0, unroll=8)

        @pl.when(blk == n_used - 1)
        def _():
            wait_scatter()


def _moe_experts(h, block_expert, row_src, n_used, w1, w2, layer):
    t, d = h.shape
    n_blocks = block_expert.shape[0]
    once = pl.Buffered(1)
    grid_spec = pltpu.PrefetchScalarGridSpec(
        num_scalar_prefetch=3,
        grid=(n_blocks,),
        in_specs=[
            pl.BlockSpec(memory_space=pl.ANY),
            pl.BlockSpec((None, None, d, 2 * D_EXPERT), lambda b, be, src, nu: (layer, be[b], 0, 0),
                         pipeline_mode=once),
            pl.BlockSpec((None, None, D_EXPERT, d), lambda b, be, src, nu: (layer, be[b], 0, 0),
                         pipeline_mode=once),
        ],
        out_specs=pl.BlockSpec(memory_space=pl.ANY),
        scratch_shapes=[pltpu.VMEM((2, MOE_BLOCK, d), F32), pltpu.VMEM((MOE_BLOCK, d), F32),
                        pltpu.VMEM((d, 2 * D_EXPERT), BF16), pltpu.VMEM((D_EXPERT, d), BF16),
                        pltpu.SemaphoreType.DMA((2,)), pltpu.SemaphoreType.DMA(())],
    )
    return pl.pallas_call(
        _moe_kernel,
        grid_spec=grid_spec,
        out_shape=jax.ShapeDtypeStruct((t + MOE_BLOCK, TOP_K * d), F32),
        compiler_params=pltpu.CompilerParams(dimension_semantics=("arbitrary",), vmem_limit_bytes=MOE_VMEM_LIMIT),
        name="moe_experts",
    )(block_expert, row_src, n_used, h, w1, w2)


def _moe_plan(expert):
    t = expert.shape[1]
    m = t * TOP_K
    e_flat = expert.T.reshape(-1)
    onehot = (e_flat[:, None] == jnp.arange(N_EXPERTS, dtype=jnp.int32)[None, :]).astype(jnp.int32)
    csum = jnp.cumsum(onehot, axis=0)
    counts = csum[-1]
    padded = (counts + MOE_BLOCK - 1) // MOE_BLOCK * MOE_BLOCK
    pend = jnp.cumsum(padded)
    pstart = pend - padded
    dest = jnp.sum(onehot * (pstart[None, :] + csum - 1), axis=1)
    m_pad = -(-m // MOE_BLOCK) * MOE_BLOCK + N_EXPERTS * MOE_BLOCK
    n_blocks = m_pad // MOE_BLOCK
    spare = (t + jnp.arange(m_pad, dtype=jnp.int32) % MOE_BLOCK) * TOP_K
    row_src = spare.at[dest].set(jnp.arange(m, dtype=jnp.int32))
    block_start = jnp.arange(n_blocks, dtype=jnp.int32) * MOE_BLOCK
    owner = jnp.sum((block_start[:, None] >= pend[None, :]).astype(jnp.int32), axis=1)
    block_expert = jnp.minimum(owner, N_EXPERTS - 1)
    n_used = (pend[-1] // MOE_BLOCK).astype(jnp.int32).reshape(1)
    return block_expert.astype(jnp.int32), row_src, n_used


def _combine_kernel(o_ref, w_ref, x_ref, gt_ref, fg_ref, out_ref, *, final, tile_off):
    is_ctx = (pl.program_id(0) + tile_off) == 0
    d = x_ref.shape[1]
    wts = w_ref[...]
    f = o_ref[:, 0:d] * wts[:, 0:1] + o_ref[:, d:2 * d] * wts[:, 1:2]
    gt = jnp.where(is_ctx, gt_ref[1:2, :], gt_ref[0:1, :])
    xn = x_ref[...] + gt * f
    if final:
        xn = _rms(xn) * fg_ref[...]
    out_ref[...] = xn


def _combine(o2, wts, xa, gt, final_g, final):
    t, d = xa.shape
    tm = TOKEN_TILE
    off = 1 if final else 0
    nt = t // tm - off
    return pl.pallas_call(
        functools.partial(_combine_kernel, final=final, tile_off=off),
        grid=(nt,),
        in_specs=[pl.BlockSpec((tm, 2 * d), lambda i: (i + off, 0)), pl.BlockSpec((tm, TOP_K), lambda i: (i + off, 0)),
                  pl.BlockSpec((tm, d), lambda i: (i + off, 0)), pl.BlockSpec((2, d), lambda i: (0, 0)),
                  pl.BlockSpec((1, d), lambda i: (0, 0))],
        out_specs=pl.BlockSpec((tm, d), lambda i: (i, 0)),
        out_shape=jax.ShapeDtypeStruct((nt * tm, d), F32),
        compiler_params=_cparams(("parallel",)),
        name="combine",
    )(o2, wts, xa, gt, final_g.reshape(1, d))


def _rope_tables(n):
    pos = jnp.arange(n)
    half = RET_HEAD // 2
    inv = ROPE_BASE ** (-jnp.arange(0, half, 2, dtype=F32) / half)

    def ang(p):
        a = p.astype(F32)[:, None] * inv
        return jnp.concatenate([a, a], axis=-1)

    a = jnp.concatenate([ang(pos // GRID_W), ang(pos % GRID_W)], axis=-1)
    sign = jnp.tile(jnp.concatenate([-jnp.ones((half // 2,), F32), jnp.ones((half // 2,), F32)]), 2)
    cos = jnp.concatenate([jnp.ones((CTX_LEN, RET_HEAD), F32), jnp.cos(a)], axis=0)
    sin = jnp.concatenate([jnp.zeros((CTX_LEN, RET_HEAD), F32), jnp.sin(a) * sign], axis=0)
    return cos, sin


def _pad_rows(w, rows, offset):
    out = jnp.zeros((rows,) + w.shape[1:], w.dtype)
    return lax.dynamic_update_slice_in_dim(out, w, offset, axis=0)


def kernel(x, c, ctx, c_ctx, ada_w, ada_b, norm1_g, norm2_g, w_in, shift_mu, rwkv_w0, rwkv_w2, rwkv_a0, rwkv_a2,
           rwkv_g2, rwkv_k_k, rwkv_k_a, rwkv_r_k, rwkv_lnx_w, rwkv_lnx_b, ret_log2_decay, ret_gn_w, w_out,
           router_w, router_b, moe_w1, moe_w2, final_g):
    b, n, d = x.shape
    assert b == 1 and d == D_MODEL and ctx.shape[1] == CTX_LEN and n % TOKEN_TILE == 0
    depth = ada_w.shape[0]
    w = RWKV_WIDTH

    xa = jnp.concatenate([ctx[0], x[0]], axis=0)
    cct = jnp.stack([c[0], c_ctx], axis=1)
    mod = _ada_mod(cct, ada_w, ada_b).reshape(depth, 2, 6, d)
    cos, sin = _rope_tables(n)
    rwt = router_w.T

    out = None
    for l in range(depth):
        last = l == depth - 1
        sh1, sc1, gt1, sh2, sc2, gt2 = (mod[l, :, i, :] for i in range(6))
        h = _norm_mod(xa, norm1_g[l], sc1, sh1)
        w_pad = jnp.concatenate(
            [w_in[l, :, :RWKV_IN], jnp.zeros((d, RWKV_PAD - RWKV_IN), F32), w_in[l, :, RWKV_IN:]], axis=1)
        p = _matmul(h, w_pad.astype(BF16), BF16)

        mu = jnp.pad(shift_mu[l], ((0, 0), (0, RWKV_PAD - RWKV_IN)))
        w2p = jnp.stack([_pad_rows(rwkv_w2[l, 0], LANE, 0), _pad_rows(rwkv_w2[l, 1], LANE, DECAY_LORA)])
        a2p = jnp.stack([_pad_rows(rwkv_a2[l, 0], LANE, 0), _pad_rows(rwkv_a2[l, 1], LANE, ICLR_LORA)])
        g2p = _pad_rows(rwkv_g2[l], 2 * LANE, 0)
        r, v, kk, g, bonus, lw, kd, bb = _rwkv_prep(
            p, mu, w2p.astype(BF16), a2p.astype(BF16), g2p.astype(BF16), rwkv_w0[l], rwkv_a0[l],
            rwkv_k_k[l].reshape(1, w), rwkv_k_a[l].reshape(1, w), rwkv_r_k[l].reshape(1, w))
        ya = _rwkv_scan(r, v, kk, lw, kd, bb)

        l2d = jnp.broadcast_to(ret_log2_decay[l].astype(F32)[:, :, None], (2, RET_HEADS, LANE))
        yb = _retention(p, cos, sin, l2d, ret_gn_w[l])

        xa, h2, scores_t = _mix_out(ya, bonus, g, yb, xa, w_out[l].astype(BF16), rwkv_lnx_w[l], rwkv_lnx_b[l],
                                    gt1, norm2_g[l], sc2, sh2, rwt)
        expert, gate = _route(scores_t, router_b)
        block_expert, row_src, n_used = _moe_plan(expert)
        o2 = _moe_experts(h2, block_expert, row_src, n_used, moe_w1, moe_w2, l)
        res = _combine(o2, gate.T, xa, gt2, final_g, last)
        if last:
            out = res
        else:
            xa = res
    return out[None]
```

```python
import functools
import math

import jax
import jax.numpy as jnp
from jax import lax
from jax.experimental import pallas as pl
from jax.experimental.pallas import tpu as pltpu

F32 = jnp.float32
BF16 = jnp.bfloat16
HIGHEST = lax.Precision.HIGHEST

D_MODEL = 2048
CTX_LEN = 256
GRID_W = 64
NORM_EPS = 1e-6

RWKV_WIDTH = 1024
RWKV_HEAD = 64
DECAY_LORA = 64
ICLR_LORA = 64
GATE_LORA = 160
LNX_EPS = 64e-5
RWKV_IN = 3 * RWKV_WIDTH + 2 * DECAY_LORA + 2 * ICLR_LORA + GATE_LORA
LORA_PAD = 512
RWKV_PAD = 3 * RWKV_WIDTH + LORA_PAD

RET_WIDTH = 1024
RET_HEAD = 128
RET_HEADS = 8
RET_CHUNK = 128
ROPE_BASE = 10000.0
P_PAD = RWKV_PAD + 4 * RET_WIDTH

N_EXPERTS = 32
N_GROUPS = 4
EXPERTS_PER_GROUP = 8
TOP_K = 2
D_EXPERT = 1024
MOE_BLOCK = 128

LANE = 128
TOKEN_TILE = 256
RWKV_CHUNK = 64
VMEM_LIMIT = 48 * 1024 * 1024
MOE_VMEM_LIMIT = 56 * 1024 * 1024


def _cparams(sem):
    return pltpu.CompilerParams(dimension_semantics=sem, vmem_limit_bytes=VMEM_LIMIT)


def _dot(a, b, precision=None):
    return jnp.dot(a, b, preferred_element_type=F32, precision=precision)


def _dot_nt(a, b, precision=None):
    return lax.dot_general(a, b, (((1,), (1,)), ((), ())), preferred_element_type=F32, precision=precision)


def _dot_tn(a, b, precision=None):
    return lax.dot_general(a, b, (((0,), (0,)), ((), ())), preferred_element_type=F32, precision=precision)


def _split_bf16(x):
    hi = x.astype(BF16)
    lo = (x - hi.astype(F32)).astype(BF16)
    return hi, lo


def _split3_bf16(x):
    hi = x.astype(BF16)
    rest = x - hi.astype(F32)
    mid = rest.astype(BF16)
    return hi, mid, (rest - mid.astype(F32)).astype(BF16)


def _head_sum(x, head):
    ri = lax.broadcasted_iota(jnp.int32, (LANE, LANE), 0) // head
    ci = lax.broadcasted_iota(jnp.int32, (LANE, LANE), 1) // head
    bd = (ri == ci).astype(BF16)
    outs = []
    for j in range(x.shape[1] // LANE):
        hi, lo = _split_bf16(x[:, j * LANE:(j + 1) * LANE])
        outs.append(_dot(hi, bd) + _dot(lo, bd))
    return outs[0] if len(outs) == 1 else jnp.concatenate(outs, axis=1)


def _ada_kernel(cct_ref, w_ref, b_ref, o_ref):
    k = pl.program_id(2)

    @pl.when(k == 0)
    def _():
        o_ref[0] = jnp.broadcast_to(b_ref[0], o_ref.shape[1:])

    s = cct_ref[...]
    s = s * jax.nn.sigmoid(s)
    w = w_ref[0]
    acc0 = jnp.sum(s[:, 0:1] * w, axis=0, keepdims=True)
    acc1 = jnp.sum(s[:, 1:2] * w, axis=0, keepdims=True)
    o_ref[0] += jnp.concatenate([acc0, acc1], axis=0)


def _ada_mod(cct, ada_w, ada_b):
    depth, d, n6 = ada_w.shape
    tk, tn = 512, 2048
    return pl.pallas_call(
        _ada_kernel,
        grid=(depth, n6 // tn, d // tk),
        in_specs=[
            pl.BlockSpec((tk, 2), lambda l, j, k: (k, 0)),
            pl.BlockSpec((1, tk, tn), lambda l, j, k: (l, k, j)),
            pl.BlockSpec((1, 1, tn), lambda l, j, k: (l, 0, j)),
        ],
        out_specs=pl.BlockSpec((1, 2, tn), lambda l, j, k: (l, 0, j)),
        out_shape=jax.ShapeDtypeStruct((depth, 2, n6), F32),
        compiler_params=_cparams(("parallel", "parallel", "arbitrary")),
        name="ada_mod",
    )(cct, ada_w, ada_b.reshape(depth, 1, n6))


def _rms(x):
    return x * lax.rsqrt(jnp.mean(x * x, axis=-1, keepdims=True) + NORM_EPS)


def _norm_mod_kernel(x_ref, g_ref, sc_ref, sh_ref, o_ref):
    is_ctx = pl.program_id(0) == 0
    y = _rms(x_ref[...]) * g_ref[...]
    sc = jnp.where(is_ctx, sc_ref[1:2, :], sc_ref[0:1, :])
    sh = jnp.where(is_ctx, sh_ref[1:2, :], sh_ref[0:1, :])
    o_ref[...] = (y * (1.0 + sc) + sh).astype(o_ref.dtype)


def _norm_mod(xa, g, sc, sh):
    t, d = xa.shape
    row = pl.BlockSpec((1, d), lambda i: (0, 0))
    two = pl.BlockSpec((2, d), lambda i: (0, 0))
    return pl.pallas_call(
        _norm_mod_kernel,
        grid=(t // TOKEN_TILE,),
        in_specs=[pl.BlockSpec((TOKEN_TILE, d), lambda i: (i, 0)), row, two, two],
        out_specs=pl.BlockSpec((TOKEN_TILE, d), lambda i: (i, 0)),
        out_shape=jax.ShapeDtypeStruct((t, d), BF16),
        compiler_params=_cparams(("parallel",)),
        name="norm_mod",
    )(xa, g.reshape(1, d), sc, sh)


def _matmul_kernel(a_ref, b_ref, o_ref):
    o_ref[...] = _dot(a_ref[...], b_ref[...]).astype(o_ref.dtype)


def _row_tile(t, cap):
    tm = TOKEN_TILE
    for cand in range(TOKEN_TILE, cap + 1, TOKEN_TILE):
        if t % cand == 0:
            tm = cand
    return tm


def _matmul(a, b, out_dtype):
    m, k = a.shape
    n = b.shape[1]
    tm, tn = _row_tile(m, 1280), 512
    return pl.pallas_call(
        _matmul_kernel,
        grid=(m // tm, n // tn),
        in_specs=[pl.BlockSpec((tm, k), lambda i, j: (i, 0)), pl.BlockSpec((k, tn), lambda i, j: (0, j))],
        out_specs=pl.BlockSpec((tm, tn), lambda i, j: (i, j)),
        out_shape=jax.ShapeDtypeStruct((m, n), out_dtype),
        compiler_params=_cparams(("parallel", "arbitrary")),
        name="in_proj",
    )(a, b)


def _rwkv_prep_kernel(pm_ref, pp_ref, pn_ref, mu_ref, w2_ref, a2_ref, g2_ref, w0_ref, a0_ref, kk_ref, ka_ref,
                      rk_ref, r_out, v_out, kk_out, g_out, bonus_out, lw_out, kd_out, b_out):
    i = pl.program_id(0)
    nt = pl.num_programs(0)
    is_ctx = i == 0
    tm = TOKEN_TILE
    row = lax.broadcasted_iota(jnp.int32, (tm, 1), 0)
    in_row = row & (GRID_W - 1)
    edge_lo = jnp.where(is_ctx, row, in_row) == 0
    edge_hi = jnp.where(is_ctx, row - (tm - 1), in_row - (GRID_W - 1)) == 0
    no_up = jnp.logical_or(is_ctx, i == 1)
    no_down = jnp.logical_or(is_ctx, i == nt - 1)
    x_gate = jnp.where(is_ctx, 0.0, 1.0)

    def shifted(c0, c1):
        p = pm_ref[:, c0:c1].astype(F32)
        mu = mu_ref[:, c0:c1]
        prev = jnp.where(edge_lo, 0.0, pltpu.roll(p, 1, axis=0))
        nxt = jnp.where(edge_hi, 0.0, pltpu.roll(p, tm - 1, axis=0))
        out = p + mu[0:1] * (prev - p) + mu[1:2] * (nxt - p)
        above = jnp.where(no_up, 0.0, pp_ref[:, c0:c1].astype(F32))
        below = jnp.where(no_down, 0.0, pn_ref[:, c0:c1].astype(F32))
        up = jnp.concatenate([above, p[:tm - GRID_W]], axis=0)
        down = jnp.concatenate([p[GRID_W:], below], axis=0)
        return out + x_gate * (mu[2:3] * (up - p) + mu[3:4] * (down - p))

    w = RWKV_WIDTH
    ul = shifted(3 * w, 3 * w + LORA_PAD)
    twd = jnp.tanh(ul[:, 0:LANE]).astype(BF16)
    ad = ul[:, LANE:2 * LANE].astype(BF16)
    sg = jax.nn.sigmoid(ul[:, 2 * LANE:4 * LANE]).astype(BF16)
    g_out[...] = _dot(sg, g2_ref[...]).astype(g_out.dtype)

    r = shifted(0, w)
    k = shifted(w, 2 * w)
    v = shifted(2 * w, 3 * w)
    r_out[...] = r.astype(r_out.dtype)
    v_out[...] = v.astype(v_out.dtype)
    kkr = k * kk_ref[...]
    nrm = jnp.sqrt(_head_sum(kkr * kkr, RWKV_HEAD))
    kk = kkr / jnp.maximum(nrm, 1e-12)
    kk_out[...] = kk.astype(kk_out.dtype)
    ksum = jnp.zeros_like(k)
    for d in range(2):
        z = w0_ref[d:d + 1, :] + _dot(twd, w2_ref[d])
        lw_out[d] = -jax.nn.sigmoid(z) * math.exp(-0.5)
        a = jax.nn.sigmoid(a0_ref[d:d + 1, :] + _dot(ad, a2_ref[d]))
        kd = k * (1.0 + (a - 1.0) * ka_ref[...])
        kd_out[d] = kd.astype(kd_out.dtype)
        b_out[d] = (a * kk).astype(b_out.dtype)
        ksum = ksum + kd
    bonus = _head_sum(r * ksum * rk_ref[...], RWKV_HEAD) * v
    bonus_out[...] = bonus.astype(bonus_out.dtype)


def _rwkv_prep(p, mu, w2p, a2p, g2p, w0, a0, k_k, k_a, r_k):
    t = p.shape[0]
    nt = t // TOKEN_TILE
    per_tile = TOKEN_TILE // GRID_W
    n_rows = t // GRID_W
    w = RWKV_WIDTH
    full = lambda shape: pl.BlockSpec(shape, lambda i: (0,) * len(shape))
    tok = pl.BlockSpec((TOKEN_TILE, w), lambda i: (i, 0))
    tok2 = pl.BlockSpec((2, TOKEN_TILE, w), lambda i: (0, i, 0))
    one = jax.ShapeDtypeStruct((t, w), BF16)
    return pl.pallas_call(
        _rwkv_prep_kernel,
        grid=(nt,),
        in_specs=[
            pl.BlockSpec((TOKEN_TILE, RWKV_PAD), lambda i: (i, 0)),
            pl.BlockSpec((GRID_W, RWKV_PAD), lambda i: (jnp.maximum(i * per_tile - 1, 0), 0)),
            pl.BlockSpec((GRID_W, RWKV_PAD), lambda i: (jnp.minimum((i + 1) * per_tile, n_rows - 1), 0)),
            full((4, RWKV_PAD)), full((2, LANE, w)), full((2, LANE, w)), full((2 * LANE, w)),
            full((2, w)), full((2, w)), full((1, w)), full((1, w)), full((1, w)),
        ],
        out_specs=[tok, tok, tok, tok, tok, tok2, tok2, tok2],
        out_shape=[one, one, one, one, one,
                   jax.ShapeDtypeStruct((2, t, w), F32),
                   jax.ShapeDtypeStruct((2, t, w), BF16),
                   jax.ShapeDtypeStruct((2, t, w), BF16)],
        compiler_params=_cparams(("parallel",)),
        name="rwkv_prep",
    )(p, p, p, mu, w2p, a2p, g2p, w0, a0, k_k, k_a, r_k)


RWKV_TILE_PAIRS = 4


def _stack_heads(x):
    lane = lax.broadcasted_iota(jnp.int32, (1, LANE), 1)
    zero = jnp.zeros_like(x)
    return jnp.concatenate([jnp.where(lane < RWKV_HEAD, x, zero), jnp.where(lane >= RWKV_HEAD, x, zero)], axis=0)


def _rwkv_scan_kernel(r_ref, v_ref, kk_ref, lw_ref, kd_ref, b_ref, y_ref, m_ref, lq_s, yl_s, ml_s, el_s):
    d = pl.program_id(0)
    s = pl.program_id(2)
    sign = 1 - 2 * d
    c = RWKV_CHUNK
    tm = TOKEN_TILE
    n_chunks = tm // c
    n_pairs = r_ref.shape[1] // LANE

    @pl.when(s == 0)
    def _():
        m_ref[...] = jnp.zeros_like(m_ref)

    ri = lax.broadcasted_iota(jnp.int32, (tm, tm), 0)
    ci = lax.broadcasted_iota(jnp.int32, (tm, tm), 1)
    same_chunk = (ri // c) == (ci // c)
    along = ((ci - ri) * sign) <= 0
    tri = jnp.where(jnp.logical_and(same_chunk, along), 1.0, 0.0).astype(BF16)
    lw = lw_ref[...]
    lw3 = _split3_bf16(lw)
    sums = sum(_dot(jnp.concatenate([tri, jnp.where(same_chunk, 1.0, 0.0).astype(BF16)], axis=0), piece)
               for piece in lw3)
    lc = sums[:tm]
    lend = sums[tm:]
    e_neg = jnp.exp(-lc)
    e_end = jnp.exp(lend - lc)
    kd = kd_ref[...].astype(F32)
    b = b_ref[...].astype(F32)
    rl_all = r_ref[...].astype(F32) * jnp.exp(lc)
    kke_all = (kk_ref[...].astype(F32) * jnp.exp(lc - lw)).astype(BF16)
    kinv_all = (kd * e_neg).astype(BF16)
    binv_all = (b * e_neg).astype(BF16)
    kdec_all = (kd * e_end).astype(BF16)
    bdec_all = (b * e_end).astype(BF16)

    trow = lax.broadcasted_iota(jnp.int32, (c, LANE), 0)
    tcol = lax.broadcasted_iota(jnp.int32, (c, LANE), 1) & (RWKV_HEAD - 1)
    rel = (tcol - trow) * sign
    incl = rel <= 0
    strict = rel < 0
    eye = jnp.where(rel == 0, 1.0, 0.0)
    hi = lax.broadcasted_iota(jnp.int32, (LANE, LANE), 0) < RWKV_HEAD
    hj = lax.broadcasted_iota(jnp.int32, (LANE, LANE), 1) < RWKV_HEAD
    same_head = hi == hj

    units = [(cc, pr) for cc in range(n_chunks) for pr in range(n_pairs)]

    def sl(x, u):
        cc, pr = u
        return x[cc * c:(cc + 1) * c, pr * LANE:(pr + 1) * LANE]

    rl = [sl(rl_all, u) for u in units]
    kke = [sl(kke_all, u) for u in units]
    vb = [v_ref[u[0] * c:(u[0] + 1) * c, u[1] * LANE:(u[1] + 1) * LANE] for u in units]
    vst = [_stack_heads(x) for x in vb]
    n_units = len(units)
    aa = [_dot_nt(jnp.concatenate([kke[i], rl[i].astype(BF16)], axis=0),
                  jnp.concatenate([_stack_heads(sl(kinv_all, u)), _stack_heads(sl(binv_all, u))], axis=0))
          for i, u in enumerate(units)]
    a_kr = [jnp.concatenate([jnp.where(strict, x[:c, :LANE], 0.0), jnp.where(incl, x[c:, :LANE], 0.0)],
                            axis=0).astype(BF16) for x in aa]
    a_rb = [jnp.where(incl, x[c:, LANE:], 0.0).astype(BF16) for x in aa]
    pw = [jnp.where(strict, -x[:c, LANE:], 0.0) for x in aa]
    tinv = [eye + x for x in pw]
    pwb = [x.astype(BF16) for x in pw]
    pw = [_dot(x, _stack_heads(x)) for x in pwb]
    for _ in range(4):
        pwb = [x.astype(BF16) for x in pw]
        both = [_dot(jnp.concatenate([x, t.astype(BF16)], axis=0), _stack_heads(x)) for x, t in zip(pwb, tinv)]
        pw = [x[:c] for x in both]
        tinv = [t + x[c:] for t, x in zip(tinv, both)]
    tb = [(t + _dot(t.astype(BF16), _stack_heads(x.astype(BF16)))).astype(BF16) for t, x in zip(tinv, pw)]
    avr = [_dot(a, x) for a, x in zip(a_kr, vst)]
    w12 = [_dot(tb[i], jnp.concatenate([_stack_heads(kke[i]), _stack_heads(avr[i][:c].astype(BF16))], axis=1))
           .astype(BF16) for i in range(n_units)]
    corr = [_dot(a_rb[i], jnp.concatenate([_stack_heads(w12[i][:, :LANE]), _stack_heads(w12[i][:, LANE:])], axis=1))
            for i in range(n_units)]
    ones = jnp.ones((3 * c, LANE), BF16)
    for i, u in enumerate(units):
        cc, pr = u
        bw = _dot_tn(sl(bdec_all, u), w12[i])
        lq = jnp.concatenate([rl[i] - corr[i][:, :LANE], jnp.where(same_head, -bw[:, :LANE], 0.0)], axis=0)
        l_hi, l_lo = _split_bf16(lq)
        lq_s[cc, pr] = jnp.concatenate([l_hi, l_lo], axis=1)
        yl_s[cc, pr] = avr[i][c:] - corr[i][:, LANE:]
        ml_s[cc, pr] = jnp.where(same_head, _dot_tn(sl(kdec_all, u), vb[i]) - bw[:, LANE:], 0.0)
        el_s[cc, pr] = jnp.exp(_dot_tn(jnp.concatenate([sl(piece, u) for piece in lw3], axis=0), ones))

    m = [m_ref[pr] for pr in range(n_pairs)]
    for step in range(n_chunks):
        cc = jnp.where(d == 1, n_chunks - 1 - step, step)
        rows = pl.ds(pl.multiple_of(cc * c, c), c)
        for pr in range(n_pairs):
            m_hi, m_lo = _split_bf16(m[pr])
            lq = lq_s[cc, pr]
            res = _dot(lq, jnp.concatenate([m_hi, m_hi], axis=0)) + _dot(lq[:, :LANE], m_lo)
            y_ref[rows, pr * LANE:(pr + 1) * LANE] = (res[:c] + yl_s[cc, pr]).astype(y_ref.dtype)
            m[pr] = el_s[cc, pr] * m[pr] + res[c:] + ml_s[cc, pr]
    for pr in range(n_pairs):
        m_ref[pr] = m[pr]


def _rwkv_scan(r, v, kk, lw, kd, b):
    t, w = r.shape
    nt = t // TOKEN_TILE
    wb = RWKV_TILE_PAIRS * LANE
    n_chunks = TOKEN_TILE // RWKV_CHUNK
    n_pairs = RWKV_TILE_PAIRS

    def tile(d, s):
        return jnp.where(jnp.logical_and(d == 1, s > 0), nt - s, s)

    shared = pl.BlockSpec((TOKEN_TILE, wb), lambda d, j, s: (tile(d, s), j))
    per_dir = pl.BlockSpec((None, TOKEN_TILE, wb), lambda d, j, s: (d, tile(d, s), j))
    unit = lambda rows: pltpu.VMEM((n_chunks, n_pairs, rows, LANE), F32)
    return pl.pallas_call(
        _rwkv_scan_kernel,
        grid=(2, w // wb, nt),
        in_specs=[shared, shared, shared, per_dir, per_dir, per_dir],
        out_specs=per_dir,
        out_shape=jax.ShapeDtypeStruct((2, t, w), BF16),
        scratch_shapes=[pltpu.VMEM((n_pairs, LANE, LANE), F32),
                        pltpu.VMEM((n_chunks, n_pairs, RWKV_CHUNK + LANE, 2 * LANE), BF16),
                        unit(RWKV_CHUNK), unit(LANE), unit(LANE)],
        compiler_params=_cparams(("parallel", "parallel", "arbitrary")),
        name="rwkv_scan",
    )(r, v, kk, lw, kd, b)


def _rotate(t, cos, sin_signed):
    lane = lax.broadcasted_iota(jnp.int32, (1, LANE), 1)
    first = (lane & 63) < 32
    swapped = jnp.where(first, pltpu.roll(t, LANE - 32, axis=1), pltpu.roll(t, 32, axis=1))
    return t * cos + swapped * sin_signed


RET_TILE_HEADS = 2


def _ret_qk(q_ref, k_ref, cos_ref, sin_ref, rows, lanes):
    cos = cos_ref[rows, :]
    sin = sin_ref[rows, :]
    q = _rotate(q_ref[rows, lanes].astype(F32), cos, sin)
    k = _rotate(k_ref[rows, lanes].astype(F32) * (RET_HEAD ** -0.5), cos, sin)
    return q, k


def _log_gamma(l2d_ref, hh):
    lg = jnp.log1p(-jnp.exp2(-l2d_ref[hh:hh + 1, :]))
    return jnp.broadcast_to(lg, (RET_CHUNK, LANE))


def _ret_units(reverse):
    chunks = range(TOKEN_TILE // RET_CHUNK)
    for hh in range(RET_TILE_HEADS):
        for cc in (reversed(chunks) if reverse else chunks):
            yield hh, slice(cc * RET_CHUNK, (cc + 1) * RET_CHUNK), slice(hh * LANE, (hh + 1) * LANE)


def _ret_fwd_kernel(q_ref, k_ref, v_ref, cos_ref, sin_ref, l2d_ref, y_ref, s_ref):
    @pl.when(pl.program_id(1) == 0)
    def _():
        s_ref[...] = jnp.zeros_like(s_ref)

    c = RET_CHUNK
    idx = lax.broadcasted_iota(jnp.int32, (c, LANE), 0).astype(F32)
    for hh, rows, lanes in _ret_units(False):
        q, k = _ret_qk(q_ref, k_ref, cos_ref, sin_ref, rows, lanes)
        lg = _log_gamma(l2d_ref, hh)
        s = s_ref[hh]
        y_ref[rows, lanes] = _dot((q * jnp.exp((idx + 1.0) * lg)).astype(BF16), s.astype(BF16)).astype(y_ref.dtype)
        kz = (k * jnp.exp((c - 1.0 - idx) * lg)).astype(BF16)
        s_ref[hh] = s * jnp.exp(c * lg) + _dot_tn(kz, v_ref[rows, lanes])


def _ret_bwd_kernel(q_ref, k_ref, v_ref, g_ref, y1_ref, cos_ref, sin_ref, l2f_ref, l2b_ref, gn_ref, o_ref, s_ref):
    @pl.when(pl.program_id(1) == 0)
    def _():
        s_ref[...] = jnp.zeros_like(s_ref)

    c = RET_CHUNK
    rowi = lax.broadcasted_iota(jnp.int32, (c, LANE), 0)
    coli = lax.broadcasted_iota(jnp.int32, (c, LANE), 1)
    diff = (rowi - coli).astype(F32)
    idx = rowi.astype(F32)
    for hh, rows, lanes in _ret_units(True):
        q, k = _ret_qk(q_ref, k_ref, cos_ref, sin_ref, rows, lanes)
        v = v_ref[rows, lanes]
        lgf = _log_gamma(l2f_ref, hh)
        lgb = _log_gamma(l2b_ref, hh)
        dmat = jnp.where(rowi >= coli, jnp.exp(diff * lgf), jnp.exp(-diff * lgb))
        scores = (_dot_nt(q.astype(BF16), k.astype(BF16)) * dmat).astype(BF16)
        s = s_ref[hh]
        y = (y1_ref[rows, lanes].astype(F32) + _dot(scores, v)
             + _dot((q * jnp.exp((c - idx) * lgb)).astype(BF16), s.astype(BF16)))
        s_ref[hh] = s * jnp.exp(c * lgb) + _dot_tn((k * jnp.exp(idx * lgb)).astype(BF16), v)
        yn = y * lax.rsqrt(jnp.mean(y * y, axis=-1, keepdims=True) + NORM_EPS)
        g = g_ref[rows, lanes].astype(F32)
        o_ref[rows, lanes] = (yn * gn_ref[:, lanes] * (g * jax.nn.sigmoid(g))).astype(o_ref.dtype)


def _retention(p, cos, sin, l2d, gn_w):
    t = p.shape[0]
    nt = t // TOKEN_TILE
    wb = RET_TILE_HEADS * LANE
    n_col = RET_WIDTH // wb
    base = RWKV_PAD // wb
    l2d = l2d.reshape(2, n_col, RET_TILE_HEADS, LANE)

    def rtile(s):
        return jnp.where(s > 0, nt - s, s)

    blk = (TOKEN_TILE, wb)
    rope = (TOKEN_TILE, LANE)
    dec = lambda d: pl.BlockSpec((None, None, RET_TILE_HEADS, LANE), lambda j, s: (d, j, 0, 0))
    col = lambda part: pl.BlockSpec(blk, lambda j, s: (s, base + part * n_col + j))
    rcol = lambda part: pl.BlockSpec(blk, lambda j, s: (rtile(s), base + part * n_col + j))
    scratch = [pltpu.VMEM((RET_TILE_HEADS, RET_HEAD, RET_HEAD), F32)]
    y1 = pl.pallas_call(
        _ret_fwd_kernel,
        grid=(n_col, nt),
        in_specs=[col(0), col(1), col(2), pl.BlockSpec(rope, lambda j, s: (s, 0)),
                  pl.BlockSpec(rope, lambda j, s: (s, 0)), dec(0)],
        out_specs=pl.BlockSpec(blk, lambda j, s: (s, j)),
        out_shape=jax.ShapeDtypeStruct((t, RET_WIDTH), BF16),
        scratch_shapes=scratch,
        compiler_params=_cparams(("parallel", "arbitrary")),
        name="ret_fwd",
    )(p, p, p, cos, sin, l2d)
    return pl.pallas_call(
        _ret_bwd_kernel,
        grid=(n_col, nt),
        in_specs=[rcol(0), rcol(1), rcol(2), rcol(3), pl.BlockSpec(blk, lambda j, s: (rtile(s), j)),
                  pl.BlockSpec(rope, lambda j, s: (rtile(s), 0)), pl.BlockSpec(rope, lambda j, s: (rtile(s), 0)),
                  dec(0), dec(1), pl.BlockSpec((1, wb), lambda j, s: (0, j))],
        out_specs=pl.BlockSpec(blk, lambda j, s: (rtile(s), j)),
        out_shape=jax.ShapeDtypeStruct((t, RET_WIDTH), BF16),
        scratch_shapes=scratch,
        compiler_params=_cparams(("parallel", "arbitrary")),
        name="ret_bwd",
    )(p, p, p, p, y1, cos, sin, l2d, l2d, gn_w.reshape(1, RET_WIDTH))


def _mix_out_kernel(ya_ref, bonus_ref, g_ref, yb_ref, x_ref, wo_ref, lnw_ref, lnb_ref, gt_ref, n2_ref, sc_ref,
                    sh_ref, rw_ref, x_out, h_out, s_out):
    is_ctx = pl.program_id(0) == 0
    y = ya_ref[0].astype(F32) + ya_ref[1].astype(F32)
    inv_n = 1.0 / RWKV_HEAD
    mu = _head_sum(y, RWKV_HEAD) * inv_n
    yc = y - mu
    var = _head_sum(yc * yc, RWKV_HEAD) * inv_n
    yn = yc * lax.rsqrt(var + LNX_EPS)
    ya = (yn * lnw_ref[...] + lnb_ref[...] + bonus_ref[...].astype(F32)) * g_ref[...].astype(F32)
    z = _dot(ya.astype(BF16), wo_ref[0:RWKV_WIDTH, :]) + _dot(yb_ref[...], wo_ref[RWKV_WIDTH:, :])
    pick = lambda ref: jnp.where(is_ctx, ref[1:2, :], ref[0:1, :])
    xn = x_ref[...] + pick(gt_ref) * z
    x_out[...] = xn
    h = _rms(xn) * n2_ref[...] * (1.0 + pick(sc_ref)) + pick(sh_ref)
    h_out[...] = h
    s_out[...] = jax.nn.sigmoid(_dot_nt(rw_ref[...], h, precision=HIGHEST))


def _mix_out(ya, bonus, g, yb, xa, wo, lnw, lnb, gt, n2, sc, sh, rwt):
    t, d = xa.shape
    w = RWKV_WIDTH
    tm = TOKEN_TILE
    row = lambda n: pl.BlockSpec((1, n), lambda i: (0, 0))
    two = pl.BlockSpec((2, d), lambda i: (0, 0))
    tok = lambda n: pl.BlockSpec((tm, n), lambda i: (i, 0))
    return pl.pallas_call(
        _mix_out_kernel,
        grid=(t // tm,),
        in_specs=[pl.BlockSpec((2, tm, w), lambda i: (0, i, 0)), tok(w), tok(w), tok(w), tok(d),
                  pl.BlockSpec((d, d), lambda i: (0, 0)), row(w), row(w), two, row(d), two, two,
                  pl.BlockSpec((N_EXPERTS, d), lambda i: (0, 0))],
        out_specs=[tok(d), tok(d), pl.BlockSpec((N_EXPERTS, tm), lambda i: (0, i))],
        out_shape=[jax.ShapeDtypeStruct((t, d), F32), jax.ShapeDtypeStruct((t, d), F32),
                   jax.ShapeDtypeStruct((N_EXPERTS, t), F32)],
        compiler_params=_cparams(("parallel",)),
        name="mix_out",
    )(ya, bonus, g, yb, xa, wo, lnw.reshape(1, w), lnb.reshape(1, w), gt, n2.reshape(1, d), sc, sh, rwt)


def _first_argmax(x, idx, n):
    m = jnp.max(x, axis=0, keepdims=True)
    return m, jnp.min(jnp.where(x == m, idx, n), axis=0, keepdims=True)


def _route_kernel(s_ref, rb_ref, e_out, w_out):
    epg = EXPERTS_PER_GROUP
    tn = s_ref.shape[1]
    idx = lax.broadcasted_iota(jnp.int32, (epg, tn), 0)
    neg = -jnp.inf
    best = None
    for gi in range(N_GROUPS):
        sc = s_ref[gi * epg:(gi + 1) * epg, :]
        bi = sc + rb_ref[gi * epg:(gi + 1) * epg, :]
        m1, i1 = _first_argmax(bi, idx, epg)
        m2 = jnp.max(jnp.where(idx == i1, neg, bi), axis=0, keepdims=True)
        gs = m1 + m2
        if best is None:
            best, g_sel, sc_sel, bi_sel = gs, jnp.zeros((1, tn), jnp.int32), sc, bi
        else:
            better = gs > best
            best = jnp.where(better, gs, best)
            g_sel = jnp.where(better, gi, g_sel)
            sc_sel = jnp.where(better, sc, sc_sel)
            bi_sel = jnp.where(better, bi, bi_sel)
    _, l1 = _first_argmax(bi_sel, idx, epg)
    _, l2 = _first_argmax(jnp.where(idx == l1, neg, bi_sel), idx, epg)
    w1 = jnp.sum(jnp.where(idx == l1, sc_sel, 0.0), axis=0, keepdims=True)
    w2 = jnp.sum(jnp.where(idx == l2, sc_sel, 0.0), axis=0, keepdims=True)
    tot = w1 + w2
    e_out[...] = jnp.concatenate([g_sel * epg + l1, g_sel * epg + l2], axis=0)
    w_out[...] = jnp.concatenate([w1 / tot, w2 / tot], axis=0)


def _route(scores_t, router_b):
    t = scores_t.shape[1]
    tn = _row_tile(t, 1280)
    return pl.pallas_call(
        _route_kernel,
        grid=(t // tn,),
        in_specs=[pl.BlockSpec((N_EXPERTS, tn), lambda i: (0, i)), pl.BlockSpec((N_EXPERTS, 1), lambda i: (0, 0))],
        out_specs=[pl.BlockSpec((TOP_K, tn), lambda i: (0, i)), pl.BlockSpec((TOP_K, tn), lambda i: (0, i))],
        out_shape=[jax.ShapeDtypeStruct((TOP_K, t), jnp.int32), jax.ShapeDtypeStruct((TOP_K, t), F32)],
        compiler_params=_cparams(("parallel",)),
        name="route",
    )(scores_t, router_b.reshape(N_EXPERTS, 1))


def _moe_kernel(be_ref, src_ref, nused_ref, h_hbm, w1_ref, w2_ref, o_hbm, x0, x1, y0, y1, w1b, w2b, gsem, ssem):
    blk = pl.program_id(0)
    n_used = nused_ref[0]
    n_blocks = pl.num_programs(0)
    d = x0.shape[1]
    last_tok = h_hbm.shape[0] - 1
    assert TOP_K == 2

    def gather_copy(b, r, xdst, sem):
        tok = jnp.minimum(jnp.right_shift(src_ref[b * MOE_BLOCK + r], 1), last_tok)
        return pltpu.make_async_copy(h_hbm.at[pl.ds(tok, 1)], xdst.at[pl.ds(r, 1)], sem)

    def wait_rows(xbuf, sem):
        pltpu.make_async_copy(h_hbm.at[pl.ds(0, MOE_BLOCK)], xbuf, sem).wait()

    def wait_results(ybuf, sem):
        pltpu.make_async_copy(ybuf, o_hbm.at[pl.ds(0, MOE_BLOCK), pl.ds(0, d)], sem).wait()

    def scatter_rows(b, ysrc, sem, to_spare):
        for r in range(MOE_BLOCK):
            src = src_ref[b * MOE_BLOCK + r]
            tok = jnp.where(to_spare, last_tok + 1 + r, jnp.right_shift(src, 1))
            lane0 = pl.multiple_of(jnp.where(to_spare, 0, (src & 1) * d), d)
            pltpu.make_async_copy(ysrc.at[pl.ds(r, 1)], o_hbm.at[pl.ds(tok, 1), pl.ds(lane0, d)], sem).start()

    def step(par, xcur, xnext, ycur, yprev):
        first = blk == 0

        @pl.when(first)
        def _():
            yprev[...] = jnp.zeros_like(yprev)
            pltpu.make_async_copy(yprev, o_hbm.at[pl.ds(last_tok + 1, MOE_BLOCK), pl.ds(d, d)], ssem.at[par]).start()

            def body(r, carry):
                gather_copy(blk, r, xcur, gsem.at[par]).start()
                return carry

            lax.fori_loop(0, MOE_BLOCK, body, 0, unroll=8)

        @pl.when(jnp.logical_or(first, be_ref[blk] != be_ref[jnp.maximum(blk - 1, 0)]))
        def _():
            w1b[...] = w1_ref[...].astype(BF16)
            w2b[...] = w2_ref[...].astype(BF16)

        wait_rows(xcur, gsem.at[par])
        nxt = jnp.minimum(blk + 1, n_blocks - 1)
        for r in range(MOE_BLOCK):
            gather_copy(nxt, r, xnext, gsem.at[1 - par]).start()
        scatter_rows(jnp.maximum(blk - 1, 0), yprev, ssem.at[1 - par], first)
        gu = _dot(xcur[...].astype(BF16), w1b[...])
        gate = gu[:, :D_EXPERT]
        act = (gate * jax.nn.sigmoid(gate) * gu[:, D_EXPERT:]).astype(BF16)
        wait_results(ycur, ssem.at[par])
        ycur[...] = _dot(act, w2b[...])

        @pl.when(blk == n_used - 1)
        def _():
            scatter_rows(blk, ycur, ssem.at[par], False)
            wait_results(ycur, ssem.at[par])
            wait_results(yprev, ssem.at[1 - par])
            wait_rows(xnext, gsem.at[1 - par])

    @pl.when(jnp.logical_and(blk < n_used, (blk & 1) == 0))
    def _():
        step(0, x0, x1, y0, y1)

    @pl.when(jnp.logical_and(blk < n_used, (blk & 1) == 1))
    def _():
        step(1, x1, x0, y1, y0)


def _moe_experts(h, block_expert, row_src, n_used, w1, w2, layer):
    t, d = h.shape
    n_blocks = block_expert.shape[0]
    once = pl.Buffered(1)
    grid_spec = pltpu.PrefetchScalarGridSpec(
        num_scalar_prefetch=3,
        grid=(n_blocks,),
        in_specs=[
            pl.BlockSpec(memory_space=pl.ANY),
            pl.BlockSpec((None, None, d, 2 * D_EXPERT), lambda b, be, src, nu: (layer, be[b], 0, 0),
                         pipeline_mode=once),
            pl.BlockSpec((None, None, D_EXPERT, d), lambda b, be, src, nu: (layer, be[b], 0, 0),
                         pipeline_mode=once),
        ],
        out_specs=pl.BlockSpec(memory_space=pl.ANY),
        scratch_shapes=[pltpu.VMEM((MOE_BLOCK, d), F32)] * 4 + [
            pltpu.VMEM((d, 2 * D_EXPERT), BF16), pltpu.VMEM((D_EXPERT, d), BF16),
            pltpu.SemaphoreType.DMA((2,)), pltpu.SemaphoreType.DMA((2,))],
    )
    return pl.pallas_call(
        _moe_kernel,
        grid_spec=grid_spec,
        out_shape=jax.ShapeDtypeStruct((t + MOE_BLOCK, TOP_K * d), F32),
        compiler_params=pltpu.CompilerParams(dimension_semantics=("arbitrary",), vmem_limit_bytes=MOE_VMEM_LIMIT),
        name="moe_experts",
    )(block_expert, row_src, n_used, h, w1, w2)


def _moe_plan(expert):
    t = expert.shape[1]
    m = t * TOP_K
    e_flat = expert.T.reshape(-1)
    onehot = (e_flat[:, None] == jnp.arange(N_EXPERTS, dtype=jnp.int32)[None, :]).astype(jnp.int32)
    csum = jnp.cumsum(onehot, axis=0)
    counts = csum[-1]
    padded = (counts + MOE_BLOCK - 1) // MOE_BLOCK * MOE_BLOCK
    pend = jnp.cumsum(padded)
    pstart = pend - padded
    dest = jnp.sum(onehot * (pstart[None, :] + csum - 1), axis=1)
    m_pad = -(-m // MOE_BLOCK) * MOE_BLOCK + N_EXPERTS * MOE_BLOCK
    n_blocks = m_pad // MOE_BLOCK
    spare = (t + jnp.arange(m_pad, dtype=jnp.int32) % MOE_BLOCK) * TOP_K
    row_src = spare.at[dest].set(jnp.arange(m, dtype=jnp.int32))
    block_start = jnp.arange(n_blocks, dtype=jnp.int32) * MOE_BLOCK
    owner = jnp.sum((block_start[:, None] >= pend[None, :]).astype(jnp.int32), axis=1)
    block_expert = jnp.minimum(owner, N_EXPERTS - 1)
    n_used = (pend[-1] // MOE_BLOCK).astype(jnp.int32).reshape(1)
    return block_expert.astype(jnp.int32), row_src, n_used


def _combine_kernel(o_ref, w_ref, x_ref, gt_ref, fg_ref, out_ref, *, final, tile_off):
    is_ctx = (pl.program_id(0) + tile_off) == 0
    d = x_ref.shape[1]
    wts = w_ref[...]
    f = o_ref[:, 0:d] * wts[:, 0:1] + o_ref[:, d:2 * d] * wts[:, 1:2]
    gt = jnp.where(is_ctx, gt_ref[1:2, :], gt_ref[0:1, :])
    xn = x_ref[...] + gt * f
    if final:
        xn = _rms(xn) * fg_ref[...]
    out_ref[...] = xn


def _combine(o2, wts, xa, gt, final_g, final):
    t, d = xa.shape
    tm = TOKEN_TILE
    off = 1 if final else 0
    nt = t // tm - off
    return pl.pallas_call(
        functools.partial(_combine_kernel, final=final, tile_off=off),
        grid=(nt,),
        in_specs=[pl.BlockSpec((tm, 2 * d), lambda i: (i + off, 0)), pl.BlockSpec((tm, TOP_K), lambda i: (i + off, 0)),
                  pl.BlockSpec((tm, d), lambda i: (i + off, 0)), pl.BlockSpec((2, d), lambda i: (0, 0)),
                  pl.BlockSpec((1, d), lambda i: (0, 0))],
        out_specs=pl.BlockSpec((tm, d), lambda i: (i, 0)),
        out_shape=jax.ShapeDtypeStruct((nt * tm, d), F32),
        compiler_params=_cparams(("parallel",)),
        name="combine",
    )(o2, wts, xa, gt, final_g.reshape(1, d))


def _rope_tables(n):
    pos = jnp.arange(n)
    half = RET_HEAD // 2
    inv = ROPE_BASE ** (-jnp.arange(0, half, 2, dtype=F32) / half)

    def ang(p):
        a = p.astype(F32)[:, None] * inv
        return jnp.concatenate([a, a], axis=-1)

    a = jnp.concatenate([ang(pos // GRID_W), ang(pos % GRID_W)], axis=-1)
    sign = jnp.tile(jnp.concatenate([-jnp.ones((half // 2,), F32), jnp.ones((half // 2,), F32)]), 2)
    cos = jnp.concatenate([jnp.ones((CTX_LEN, RET_HEAD), F32), jnp.cos(a)], axis=0)
    sin = jnp.concatenate([jnp.zeros((CTX_LEN, RET_HEAD), F32), jnp.sin(a) * sign], axis=0)
    return cos, sin


def _pad_rows(w, rows, offset):
    out = jnp.zeros((rows,) + w.shape[1:], w.dtype)
    return lax.dynamic_update_slice_in_dim(out, w, offset, axis=0)


def kernel(x, c, ctx, c_ctx, ada_w, ada_b, norm1_g, norm2_g, w_in, shift_mu, rwkv_w0, rwkv_w2, rwkv_a0, rwkv_a2,
           rwkv_g2, rwkv_k_k, rwkv_k_a, rwkv_r_k, rwkv_lnx_w, rwkv_lnx_b, ret_log2_decay, ret_gn_w, w_out,
           router_w, router_b, moe_w1, moe_w2, final_g):
    b, n, d = x.shape
    assert b == 1 and d == D_MODEL and ctx.shape[1] == CTX_LEN and n % TOKEN_TILE == 0
    depth = ada_w.shape[0]
    w = RWKV_WIDTH

    xa = jnp.concatenate([ctx[0], x[0]], axis=0)
    cct = jnp.stack([c[0], c_ctx], axis=1)
    mod = _ada_mod(cct, ada_w, ada_b).reshape(depth, 2, 6, d)
    cos, sin = _rope_tables(n)
    rwt = router_w.T

    out = None
    for l in range(depth):
        last = l == depth - 1
        sh1, sc1, gt1, sh2, sc2, gt2 = (mod[l, :, i, :] for i in range(6))
        h = _norm_mod(xa, norm1_g[l], sc1, sh1)
        w_pad = jnp.concatenate(
            [w_in[l, :, :RWKV_IN], jnp.zeros((d, RWKV_PAD - RWKV_IN), F32), w_in[l, :, RWKV_IN:]], axis=1)
        p = _matmul(h, w_pad.astype(BF16), BF16)

        mu = jnp.pad(shift_mu[l], ((0, 0), (0, RWKV_PAD - RWKV_IN)))
        w2p = jnp.stack([_pad_rows(rwkv_w2[l, 0], LANE, 0), _pad_rows(rwkv_w2[l, 1], LANE, DECAY_LORA)])
        a2p = jnp.stack([_pad_rows(rwkv_a2[l, 0], LANE, 0), _pad_rows(rwkv_a2[l, 1], LANE, ICLR_LORA)])
        g2p = _pad_rows(rwkv_g2[l], 2 * LANE, 0)
        r, v, kk, g, bonus, lw, kd, bb = _rwkv_prep(
            p, mu, w2p.astype(BF16), a2p.astype(BF16), g2p.astype(BF16), rwkv_w0[l], rwkv_a0[l],
            rwkv_k_k[l].reshape(1, w), rwkv_k_a[l].reshape(1, w), rwkv_r_k[l].reshape(1, w))
        ya = _rwkv_scan(r, v, kk, lw, kd, bb)

        l2d = jnp.broadcast_to(ret_log2_decay[l].astype(F32)[:, :, None], (2, RET_HEADS, LANE))
        yb = _retention(p, cos, sin, l2d, ret_gn_w[l])

        xa, h2, scores_t = _mix_out(ya, bonus, g, yb, xa, w_out[l].astype(BF16), rwkv_lnx_w[l], rwkv_lnx_b[l],
                                    gt1, norm2_g[l], sc2, sh2, rwt)
        expert, gate = _route(scores_t, router_b)
        block_expert, row_src, n_used = _moe_plan(expert)
        o2 = _moe_experts(h2, block_expert, row_src, n_used, moe_w1, moe_w2, l)
        res = _combine(o2, gate.T, xa, gt2, final_g, last)
        if last:
            out = res
        else:
            xa = res
    return out[None]
```

```python
import functools
import math

import jax
import jax.numpy as jnp
from jax import lax
from jax.experimental import pallas as pl
from jax.experimental.pallas import tpu as pltpu

F32 = jnp.float32
BF16 = jnp.bfloat16
HIGHEST = lax.Precision.HIGHEST

D_MODEL = 2048
CTX_LEN = 256
GRID_W = 64
NORM_EPS = 1e-6

RWKV_WIDTH = 1024
RWKV_HEAD = 64
DECAY_LORA = 64
ICLR_LORA = 64
GATE_LORA = 160
LNX_EPS = 64e-5
RWKV_IN = 3 * RWKV_WIDTH + 2 * DECAY_LORA + 2 * ICLR_LORA + GATE_LORA
LORA_PAD = 512
RWKV_PAD = 3 * RWKV_WIDTH + LORA_PAD

RET_WIDTH = 1024
RET_HEAD = 128
RET_HEADS = 8
RET_CHUNK = 128
ROPE_BASE = 10000.0

N_EXPERTS = 32
N_GROUPS = 4
EXPERTS_PER_GROUP = 8
TOP_K = 2
D_EXPERT = 1024
MOE_BLOCK = 128

LANE = 128
TOKEN_TILE = 256
RWKV_CHUNK = 64
VMEM_LIMIT = 48 * 1024 * 1024
MOE_VMEM_LIMIT = 56 * 1024 * 1024


def _cparams(sem):
    return pltpu.CompilerParams(dimension_semantics=sem, vmem_limit_bytes=VMEM_LIMIT)


def _dot(a, b, precision=None):
    return jnp.dot(a, b, preferred_element_type=F32, precision=precision)


def _dot_nt(a, b, precision=None):
    return lax.dot_general(a, b, (((1,), (1,)), ((), ())), preferred_element_type=F32, precision=precision)


def _dot_tn(a, b, precision=None):
    return lax.dot_general(a, b, (((0,), (0,)), ((), ())), preferred_element_type=F32, precision=precision)


def _split_bf16(x):
    hi = x.astype(BF16)
    lo = (x - hi.astype(F32)).astype(BF16)
    return hi, lo


def _split3_bf16(x):
    hi = x.astype(BF16)
    rest = x - hi.astype(F32)
    mid = rest.astype(BF16)
    return hi, mid, (rest - mid.astype(F32)).astype(BF16)


def _head_sum(x, head):
    ri = lax.broadcasted_iota(jnp.int32, (LANE, LANE), 0) // head
    ci = lax.broadcasted_iota(jnp.int32, (LANE, LANE), 1) // head
    bd = (ri == ci).astype(BF16)
    outs = []
    for j in range(x.shape[1] // LANE):
        hi, lo = _split_bf16(x[:, j * LANE:(j + 1) * LANE])
        outs.append(_dot(hi, bd) + _dot(lo, bd))
    return outs[0] if len(outs) == 1 else jnp.concatenate(outs, axis=1)


def _ada_kernel(cct_ref, w_ref, b_ref, o_ref):
    k = pl.program_id(2)

    @pl.when(k == 0)
    def _():
        o_ref[0] = jnp.broadcast_to(b_ref[0], o_ref.shape[1:])

    s = cct_ref[...]
    s = s * jax.nn.sigmoid(s)
    w = w_ref[0]
    acc0 = jnp.sum(s[:, 0:1] * w, axis=0, keepdims=True)
    acc1 = jnp.sum(s[:, 1:2] * w, axis=0, keepdims=True)
    o_ref[0] += jnp.concatenate([acc0, acc1], axis=0)


def _ada_mod(cct, ada_w, ada_b):
    depth, d, n6 = ada_w.shape
    tk, tn = 512, 2048
    return pl.pallas_call(
        _ada_kernel,
        grid=(depth, n6 // tn, d // tk),
        in_specs=[
            pl.BlockSpec((tk, 2), lambda l, j, k: (k, 0)),
            pl.BlockSpec((1, tk, tn), lambda l, j, k: (l, k, j)),
            pl.BlockSpec((1, 1, tn), lambda l, j, k: (l, 0, j)),
        ],
        out_specs=pl.BlockSpec((1, 2, tn), lambda l, j, k: (l, 0, j)),
        out_shape=jax.ShapeDtypeStruct((depth, 2, n6), F32),
        compiler_params=_cparams(("parallel", "parallel", "arbitrary")),
        name="ada_mod",
    )(cct, ada_w, ada_b.reshape(depth, 1, n6))


def _rms(x):
    return x * lax.rsqrt(jnp.mean(x * x, axis=-1, keepdims=True) + NORM_EPS)


def _norm_mod_kernel(x_ref, g_ref, sc_ref, sh_ref, o_ref):
    is_ctx = pl.program_id(0) == 0
    y = _rms(x_ref[...]) * g_ref[...]
    sc = jnp.where(is_ctx, sc_ref[1:2, :], sc_ref[0:1, :])
    sh = jnp.where(is_ctx, sh_ref[1:2, :], sh_ref[0:1, :])
    o_ref[...] = (y * (1.0 + sc) + sh).astype(o_ref.dtype)


def _norm_mod(xa, g, sc, sh):
    t, d = xa.shape
    row = pl.BlockSpec((1, d), lambda i: (0, 0))
    two = pl.BlockSpec((2, d), lambda i: (0, 0))
    return pl.pallas_call(
        _norm_mod_kernel,
        grid=(t // TOKEN_TILE,),
        in_specs=[pl.BlockSpec((TOKEN_TILE, d), lambda i: (i, 0)), row, two, two],
        out_specs=pl.BlockSpec((TOKEN_TILE, d), lambda i: (i, 0)),
        out_shape=jax.ShapeDtypeStruct((t, d), BF16),
        compiler_params=_cparams(("parallel",)),
        name="norm_mod",
    )(xa, g.reshape(1, d), sc, sh)


def _matmul_kernel(a_ref, b_ref, o_ref):
    o_ref[...] = _dot(a_ref[...], b_ref[...].astype(BF16)).astype(o_ref.dtype)


def _row_tile(t, cap):
    tm = TOKEN_TILE
    for cand in range(TOKEN_TILE, cap + 1, TOKEN_TILE):
        if t % cand == 0:
            tm = cand
    return tm


def _matmul(a, b, layer, n, out_dtype):
    m, k = a.shape
    tm, tn = _row_tile(m, 1280), 512
    assert n % tn == 0 and n <= b.shape[2]
    return pl.pallas_call(
        _matmul_kernel,
        grid=(m // tm, n // tn),
        in_specs=[pl.BlockSpec((tm, k), lambda i, j: (i, 0)),
                  pl.BlockSpec((None, k, tn), lambda i, j: (layer, 0, j))],
        out_specs=pl.BlockSpec((tm, tn), lambda i, j: (i, j)),
        out_shape=jax.ShapeDtypeStruct((m, n), out_dtype),
        compiler_params=_cparams(("parallel", "arbitrary")),
        name="in_proj",
    )(a, b)


def _rwkv_prep_kernel(pm_ref, pp_ref, pn_ref, mu_ref, w2_ref, a2_ref, g2_ref, w0_ref, a0_ref, kk_ref, ka_ref,
                      rk_ref, r_out, v_out, kk_out, g_out, bonus_out, lw_out, kd_out, b_out):
    i = pl.program_id(0)
    nt = pl.num_programs(0)
    is_ctx = i == 0
    tm = TOKEN_TILE
    row = lax.broadcasted_iota(jnp.int32, (tm, 1), 0)
    in_row = row & (GRID_W - 1)
    edge_lo = jnp.where(is_ctx, row, in_row) == 0
    edge_hi = jnp.where(is_ctx, row - (tm - 1), in_row - (GRID_W - 1)) == 0
    no_up = jnp.logical_or(is_ctx, i == 1)
    no_down = jnp.logical_or(is_ctx, i == nt - 1)
    x_gate = jnp.where(is_ctx, 0.0, 1.0)

    def shifted(c0, c1):
        p = pm_ref[:, c0:c1].astype(F32)
        mu = mu_ref[:, c0:c1]
        prev = jnp.where(edge_lo, 0.0, pltpu.roll(p, 1, axis=0))
        nxt = jnp.where(edge_hi, 0.0, pltpu.roll(p, tm - 1, axis=0))
        out = p + mu[0:1] * (prev - p) + mu[1:2] * (nxt - p)
        above = jnp.where(no_up, 0.0, pp_ref[:, c0:c1].astype(F32))
        below = jnp.where(no_down, 0.0, pn_ref[:, c0:c1].astype(F32))
        up = jnp.concatenate([above, p[:tm - GRID_W]], axis=0)
        down = jnp.concatenate([p[GRID_W:], below], axis=0)
        return out + x_gate * (mu[2:3] * (up - p) + mu[3:4] * (down - p))

    w = RWKV_WIDTH
    ul = shifted(3 * w, 3 * w + LORA_PAD)
    twd = jnp.tanh(ul[:, 0:LANE]).astype(BF16)
    ad = ul[:, LANE:2 * LANE].astype(BF16)
    sg = jax.nn.sigmoid(ul[:, 2 * LANE:4 * LANE]).astype(BF16)
    g_out[...] = _dot(sg, g2_ref[...]).astype(g_out.dtype)

    r = shifted(0, w)
    k = shifted(w, 2 * w)
    v = shifted(2 * w, 3 * w)
    r_out[...] = r.astype(r_out.dtype)
    v_out[...] = v.astype(v_out.dtype)
    kkr = k * kk_ref[...]
    nrm = jnp.sqrt(_head_sum(kkr * kkr, RWKV_HEAD))
    kk = kkr / jnp.maximum(nrm, 1e-12)
    kk_out[...] = kk.astype(kk_out.dtype)
    ksum = jnp.zeros_like(k)
    for d in range(2):
        z = w0_ref[d:d + 1, :] + _dot(twd, w2_ref[d])
        lw_out[d] = -jax.nn.sigmoid(z) * math.exp(-0.5)
        a = jax.nn.sigmoid(a0_ref[d:d + 1, :] + _dot(ad, a2_ref[d]))
        kd = k * (1.0 + (a - 1.0) * ka_ref[...])
        kd_out[d] = kd.astype(kd_out.dtype)
        b_out[d] = (a * kk).astype(b_out.dtype)
        ksum = ksum + kd
    bonus = _head_sum(r * ksum * rk_ref[...], RWKV_HEAD) * v
    bonus_out[...] = bonus.astype(bonus_out.dtype)


def _rwkv_prep(p, mu, w2p, a2p, g2p, w0, a0, k_k, k_a, r_k):
    t = p.shape[0]
    nt = t // TOKEN_TILE
    per_tile = TOKEN_TILE // GRID_W
    n_rows = t // GRID_W
    w = RWKV_WIDTH
    full = lambda shape: pl.BlockSpec(shape, lambda i: (0,) * len(shape))
    tok = pl.BlockSpec((TOKEN_TILE, w), lambda i: (i, 0))
    tok2 = pl.BlockSpec((2, TOKEN_TILE, w), lambda i: (0, i, 0))
    one = jax.ShapeDtypeStruct((t, w), BF16)
    return pl.pallas_call(
        _rwkv_prep_kernel,
        grid=(nt,),
        in_specs=[
            pl.BlockSpec((TOKEN_TILE, RWKV_PAD), lambda i: (i, 0)),
            pl.BlockSpec((GRID_W, RWKV_PAD), lambda i: (jnp.maximum(i * per_tile - 1, 0), 0)),
            pl.BlockSpec((GRID_W, RWKV_PAD), lambda i: (jnp.minimum((i + 1) * per_tile, n_rows - 1), 0)),
            full((4, RWKV_PAD)), full((2, LANE, w)), full((2, LANE, w)), full((2 * LANE, w)),
            full((2, w)), full((2, w)), full((1, w)), full((1, w)), full((1, w)),
        ],
        out_specs=[tok, tok, tok, tok, tok, tok2, tok2, tok2],
        out_shape=[one, one, one, one, one,
                   jax.ShapeDtypeStruct((2, t, w), F32),
                   jax.ShapeDtypeStruct((2, t, w), BF16),
                   jax.ShapeDtypeStruct((2, t, w), BF16)],
        compiler_params=_cparams(("parallel",)),
        name="rwkv_prep",
    )(p, p, p, mu, w2p, a2p, g2p, w0, a0, k_k, k_a, r_k)


RWKV_TILE_PAIRS = 4


def _stack_heads(x):
    lane = lax.broadcasted_iota(jnp.int32, (1, LANE), 1)
    zero = jnp.zeros_like(x)
    return jnp.concatenate([jnp.where(lane < RWKV_HEAD, x, zero), jnp.where(lane >= RWKV_HEAD, x, zero)], axis=0)


def _rwkv_scan_kernel(r_ref, v_ref, kk_ref, lw_ref, kd_ref, b_ref, y_ref, m_ref, lq_s, yl_s, ml_s, el_s):
    d = pl.program_id(0)
    s = pl.program_id(2)
    sign = 1 - 2 * d
    c = RWKV_CHUNK
    tm = TOKEN_TILE
    n_chunks = tm // c
    n_pairs = r_ref.shape[1] // LANE

    @pl.when(s == 0)
    def _():
        m_ref[...] = jnp.zeros_like(m_ref)

    ri = lax.broadcasted_iota(jnp.int32, (tm, tm), 0)
    ci = lax.broadcasted_iota(jnp.int32, (tm, tm), 1)
    same_chunk = (ri // c) == (ci // c)
    along = ((ci - ri) * sign) <= 0
    tri = jnp.where(jnp.logical_and(same_chunk, along), 1.0, 0.0).astype(BF16)
    lw = lw_ref[...]
    lw3 = _split3_bf16(lw)
    lc = sum(_dot(tri, piece) for piece in lw3)
    lend = jnp.concatenate(
        [jnp.broadcast_to(jnp.where(d == 1, lc[cc * c:cc * c + 1], lc[(cc + 1) * c - 1:(cc + 1) * c]),
                          (c, lc.shape[1])) for cc in range(n_chunks)], axis=0)
    e_neg = jnp.exp(-lc)
    e_end = jnp.exp(lend - lc)
    kd = kd_ref[...].astype(F32)
    b = b_ref[...].astype(F32)
    rl_all = r_ref[...].astype(F32) * jnp.exp(lc)
    kke_all = (kk_ref[...].astype(F32) * jnp.exp(lc - lw)).astype(BF16)
    kinv_all = (kd * e_neg).astype(BF16)
    binv_all = (b * e_neg).astype(BF16)
    kdec_all = (kd * e_end).astype(BF16)
    bdec_all = (b * e_end).astype(BF16)

    trow = lax.broadcasted_iota(jnp.int32, (c, LANE), 0)
    tcol = lax.broadcasted_iota(jnp.int32, (c, LANE), 1) & (RWKV_HEAD - 1)
    rel = (tcol - trow) * sign
    incl = rel <= 0
    strict = rel < 0
    eye = jnp.where(rel == 0, 1.0, 0.0)
    hi = lax.broadcasted_iota(jnp.int32, (LANE, LANE), 0) < RWKV_HEAD
    hj = lax.broadcasted_iota(jnp.int32, (LANE, LANE), 1) < RWKV_HEAD
    same_head = hi == hj

    units = [(cc, pr) for cc in range(n_chunks) for pr in range(n_pairs)]

    def sl(x, u):
        cc, pr = u
        return x[cc * c:(cc + 1) * c, pr * LANE:(pr + 1) * LANE]

    rl = [sl(rl_all, u) for u in units]
    kke = [sl(kke_all, u) for u in units]
    vb = [v_ref[u[0] * c:(u[0] + 1) * c, u[1] * LANE:(u[1] + 1) * LANE] for u in units]
    vst = [_stack_heads(x) for x in vb]
    n_units = len(units)
    aa = [_dot_nt(jnp.concatenate([kke[i], rl[i].astype(BF16)], axis=0),
                  jnp.concatenate([_stack_heads(sl(kinv_all, u)), _stack_heads(sl(binv_all, u))], axis=0))
          for i, u in enumerate(units)]
    a_kr = [jnp.concatenate([jnp.where(strict, x[:c, :LANE], 0.0), jnp.where(incl, x[c:, :LANE], 0.0)],
                            axis=0).astype(BF16) for x in aa]
    a_rb = [jnp.where(incl, x[c:, LANE:], 0.0).astype(BF16) for x in aa]
    pw = [jnp.where(strict, -x[:c, LANE:], 0.0) for x in aa]
    tinv = [eye + x for x in pw]
    pwb = [x.astype(BF16) for x in pw]
    pw = [_dot(x, _stack_heads(x)) for x in pwb]
    for _ in range(4):
        pwb = [x.astype(BF16) for x in pw]
        both = [_dot(jnp.concatenate([x, t.astype(BF16)], axis=0), _stack_heads(x)) for x, t in zip(pwb, tinv)]
        pw = [x[:c] for x in both]
        tinv = [t + x[c:] for t, x in zip(tinv, both)]
    tb = [(t + _dot(t.astype(BF16), _stack_heads(x.astype(BF16)))).astype(BF16) for t, x in zip(tinv, pw)]
    avr = [_dot(a, x) for a, x in zip(a_kr, vst)]
    w12 = [_dot(tb[i], jnp.concatenate([_stack_heads(kke[i]), _stack_heads(avr[i][:c].astype(BF16))], axis=1))
           .astype(BF16) for i in range(n_units)]
    corr = [_dot(a_rb[i], jnp.concatenate([_stack_heads(w12[i][:, :LANE]), _stack_heads(w12[i][:, LANE:])], axis=1))
            for i in range(n_units)]
    ones = jnp.ones((3 * c, LANE), BF16)
    for i, u in enumerate(units):
        cc, pr = u
        bw = _dot_tn(sl(bdec_all, u), w12[i])
        lq = jnp.concatenate([rl[i] - corr[i][:, :LANE], jnp.where(same_head, -bw[:, :LANE], 0.0)], axis=0)
        l_hi, l_lo = _split_bf16(lq)
        lq_s[cc, pr] = jnp.concatenate([l_hi, l_lo], axis=1)
        yl_s[cc, pr] = avr[i][c:] - corr[i][:, LANE:]
        ml_s[cc, pr] = jnp.where(same_head, _dot_tn(sl(kdec_all, u), vb[i]) - bw[:, LANE:], 0.0)
        el_s[cc, pr] = jnp.exp(_dot_tn(jnp.concatenate([sl(piece, u) for piece in lw3], axis=0), ones))

    m = [m_ref[pr] for pr in range(n_pairs)]
    for step in range(n_chunks):
        cc = jnp.where(d == 1, n_chunks - 1 - step, step)
        rows = pl.ds(pl.multiple_of(cc * c, c), c)
        for pr in range(n_pairs):
            m_hi, m_lo = _split_bf16(m[pr])
            lq = lq_s[cc, pr]
            res = _dot(lq, jnp.concatenate([m_hi, m_hi], axis=0)) + _dot(lq[:, :LANE], m_lo)
            y_ref[rows, pr * LANE:(pr + 1) * LANE] = (res[:c] + yl_s[cc, pr]).astype(y_ref.dtype)
            m[pr] = el_s[cc, pr] * m[pr] + res[c:] + ml_s[cc, pr]
    for pr in range(n_pairs):
        m_ref[pr] = m[pr]


def _rwkv_scan(r, v, kk, lw, kd, b):
    t, w = r.shape
    nt = t // TOKEN_TILE
    wb = RWKV_TILE_PAIRS * LANE
    n_chunks = TOKEN_TILE // RWKV_CHUNK
    n_pairs = RWKV_TILE_PAIRS

    def tile(d, s):
        return jnp.where(jnp.logical_and(d == 1, s > 0), nt - s, s)

    shared = pl.BlockSpec((TOKEN_TILE, wb), lambda d, j, s: (tile(d, s), j))
    per_dir = pl.BlockSpec((None, TOKEN_TILE, wb), lambda d, j, s: (d, tile(d, s), j))
    unit = lambda rows: pltpu.VMEM((n_chunks, n_pairs, rows, LANE), F32)
    return pl.pallas_call(
        _rwkv_scan_kernel,
        grid=(2, w // wb, nt),
        in_specs=[shared, shared, shared, per_dir, per_dir, per_dir],
        out_specs=per_dir,
        out_shape=jax.ShapeDtypeStruct((2, t, w), BF16),
        scratch_shapes=[pltpu.VMEM((n_pairs, LANE, LANE), F32),
                        pltpu.VMEM((n_chunks, n_pairs, RWKV_CHUNK + LANE, 2 * LANE), BF16),
                        unit(RWKV_CHUNK), unit(LANE), unit(LANE)],
        compiler_params=_cparams(("parallel", "parallel", "arbitrary")),
        name="rwkv_scan",
    )(r, v, kk, lw, kd, b)


def _rotate(t, cos, sin_signed):
    lane = lax.broadcasted_iota(jnp.int32, (1, LANE), 1)
    first = (lane & 63) < 32
    swapped = jnp.where(first, pltpu.roll(t, LANE - 32, axis=1), pltpu.roll(t, 32, axis=1))
    return t * cos + swapped * sin_signed


RET_TILE_HEADS = 2


def _ret_qk(q_ref, k_ref, cos_ref, sin_ref, rows, lanes):
    cos = cos_ref[rows, :]
    sin = sin_ref[rows, :]
    q = _rotate(q_ref[rows, lanes].astype(F32), cos, sin)
    k = _rotate(k_ref[rows, lanes].astype(F32) * (RET_HEAD ** -0.5), cos, sin)
    return q, k


def _log_gamma(l2d_ref, hh):
    lg = jnp.log1p(-jnp.exp2(-l2d_ref[hh:hh + 1, :]))
    return jnp.broadcast_to(lg, (RET_CHUNK, LANE))


def _ret_units(reverse):
    chunks = range(TOKEN_TILE // RET_CHUNK)
    for hh in range(RET_TILE_HEADS):
        for cc in (reversed(chunks) if reverse else chunks):
            yield hh, slice(cc * RET_CHUNK, (cc + 1) * RET_CHUNK), slice(hh * LANE, (hh + 1) * LANE)


def _ret_fwd_kernel(q_ref, k_ref, v_ref, cos_ref, sin_ref, l2d_ref, y_ref, s_ref):
    @pl.when(pl.program_id(1) == 0)
    def _():
        s_ref[...] = jnp.zeros_like(s_ref)

    c = RET_CHUNK
    idx = lax.broadcasted_iota(jnp.int32, (c, LANE), 0).astype(F32)
    for hh, rows, lanes in _ret_units(False):
        q, k = _ret_qk(q_ref, k_ref, cos_ref, sin_ref, rows, lanes)
        lg = _log_gamma(l2d_ref, hh)
        s = s_ref[hh]
        y_ref[rows, lanes] = _dot((q * jnp.exp((idx + 1.0) * lg)).astype(BF16), s.astype(BF16)).astype(y_ref.dtype)
        kz = (k * jnp.exp((c - 1.0 - idx) * lg)).astype(BF16)
        s_ref[hh] = s * jnp.exp(c * lg) + _dot_tn(kz, v_ref[rows, lanes])


def _ret_bwd_kernel(q_ref, k_ref, v_ref, g_ref, y1_ref, cos_ref, sin_ref, l2f_ref, l2b_ref, gn_ref, o_ref, s_ref):
    @pl.when(pl.program_id(1) == 0)
    def _():
        s_ref[...] = jnp.zeros_like(s_ref)

    c = RET_CHUNK
    rowi = lax.broadcasted_iota(jnp.int32, (c, LANE), 0)
    coli = lax.broadcasted_iota(jnp.int32, (c, LANE), 1)
    diff = (rowi - coli).astype(F32)
    idx = rowi.astype(F32)
    for hh, rows, lanes in _ret_units(True):
        q, k = _ret_qk(q_ref, k_ref, cos_ref, sin_ref, rows, lanes)
        v = v_ref[rows, lanes]
        lgf = _log_gamma(l2f_ref, hh)
        lgb = _log_gamma(l2b_ref, hh)
        dmat = jnp.where(rowi >= coli, jnp.exp(diff * lgf), jnp.exp(-diff * lgb))
        scores = (_dot_nt(q.astype(BF16), k.astype(BF16)) * dmat).astype(BF16)
        s = s_ref[hh]
        y = (y1_ref[rows, lanes].astype(F32) + _dot(scores, v)
             + _dot((q * jnp.exp((c - idx) * lgb)).astype(BF16), s.astype(BF16)))
        s_ref[hh] = s * jnp.exp(c * lgb) + _dot_tn((k * jnp.exp(idx * lgb)).astype(BF16), v)
        yn = y * lax.rsqrt(jnp.mean(y * y, axis=-1, keepdims=True) + NORM_EPS)
        g = g_ref[rows, lanes].astype(F32)
        o_ref[rows, lanes] = (yn * gn_ref[:, lanes] * (g * jax.nn.sigmoid(g))).astype(o_ref.dtype)


def _retention(p, cos, sin, l2d, gn_w):
    t = p.shape[0]
    nt = t // TOKEN_TILE
    wb = RET_TILE_HEADS * LANE
    n_col = RET_WIDTH // wb
    base = 0
    l2d = l2d.reshape(2, n_col, RET_TILE_HEADS, LANE)

    def rtile(s):
        return jnp.where(s > 0, nt - s, s)

    blk = (TOKEN_TILE, wb)
    rope = (TOKEN_TILE, LANE)
    dec = lambda d: pl.BlockSpec((None, None, RET_TILE_HEADS, LANE), lambda j, s: (d, j, 0, 0))
    col = lambda part: pl.BlockSpec(blk, lambda j, s: (s, base + part * n_col + j))
    rcol = lambda part: pl.BlockSpec(blk, lambda j, s: (rtile(s), base + part * n_col + j))
    scratch = [pltpu.VMEM((RET_TILE_HEADS, RET_HEAD, RET_HEAD), F32)]
    y1 = pl.pallas_call(
        _ret_fwd_kernel,
        grid=(n_col, nt),
        in_specs=[col(0), col(1), col(2), pl.BlockSpec(rope, lambda j, s: (s, 0)),
                  pl.BlockSpec(rope, lambda j, s: (s, 0)), dec(0)],
        out_specs=pl.BlockSpec(blk, lambda j, s: (s, j)),
        out_shape=jax.ShapeDtypeStruct((t, RET_WIDTH), BF16),
        scratch_shapes=scratch,
        compiler_params=_cparams(("parallel", "arbitrary")),
        name="ret_fwd",
    )(p, p, p, cos, sin, l2d)
    return pl.pallas_call(
        _ret_bwd_kernel,
        grid=(n_col, nt),
        in_specs=[rcol(0), rcol(1), rcol(2), rcol(3), pl.BlockSpec(blk, lambda j, s: (rtile(s), j)),
                  pl.BlockSpec(rope, lambda j, s: (rtile(s), 0)), pl.BlockSpec(rope, lambda j, s: (rtile(s), 0)),
                  dec(0), dec(1), pl.BlockSpec((1, wb), lambda j, s: (0, j))],
        out_specs=pl.BlockSpec(blk, lambda j, s: (rtile(s), j)),
        out_shape=jax.ShapeDtypeStruct((t, RET_WIDTH), BF16),
        scratch_shapes=scratch,
        compiler_params=_cparams(("parallel", "arbitrary")),
        name="ret_bwd",
    )(p, p, p, p, y1, cos, sin, l2d, l2d, gn_w.reshape(1, RET_WIDTH))


def _mix_out_kernel(ya_ref, bonus_ref, g_ref, yb_ref, x_ref, wo_ref, lnw_ref, lnb_ref, gt_ref, n2_ref, sc_ref,
                    sh_ref, rw_ref, x_out, h_out, s_out):
    is_ctx = pl.program_id(0) == 0
    y = ya_ref[0].astype(F32) + ya_ref[1].astype(F32)
    inv_n = 1.0 / RWKV_HEAD
    mu = _head_sum(y, RWKV_HEAD) * inv_n
    yc = y - mu
    var = _head_sum(yc * yc, RWKV_HEAD) * inv_n
    yn = yc * lax.rsqrt(var + LNX_EPS)
    ya = (yn * lnw_ref[...] + lnb_ref[...] + bonus_ref[...].astype(F32)) * g_ref[...].astype(F32)
    z = _dot(ya.astype(BF16), wo_ref[0:RWKV_WIDTH, :]) + _dot(yb_ref[...], wo_ref[RWKV_WIDTH:, :])
    pick = lambda ref: jnp.where(is_ctx, ref[1:2, :], ref[0:1, :])
    xn = x_ref[...] + pick(gt_ref) * z
    x_out[...] = xn
    h = _rms(xn) * n2_ref[...] * (1.0 + pick(sc_ref)) + pick(sh_ref)
    h_out[...] = h
    h_hi, h_lo = _split_bf16(h)
    r_hi, r_lo = _split_bf16(rw_ref[...])
    s_out[...] = jax.nn.sigmoid(_dot(h_hi, r_hi) + _dot(h_hi, r_lo) + _dot(h_lo, r_hi))


def _mix_out(ya, bonus, g, yb, xa, wo, lnw, lnb, gt, n2, sc, sh, rwt):
    t, d = xa.shape
    w = RWKV_WIDTH
    tm = TOKEN_TILE
    row = lambda n: pl.BlockSpec((1, n), lambda i: (0, 0))
    two = pl.BlockSpec((2, d), lambda i: (0, 0))
    tok = lambda n: pl.BlockSpec((tm, n), lambda i: (i, 0))
    return pl.pallas_call(
        _mix_out_kernel,
        grid=(t // tm,),
        in_specs=[pl.BlockSpec((2, tm, w), lambda i: (0, i, 0)), tok(w), tok(w), tok(w), tok(d),
                  pl.BlockSpec((d, d), lambda i: (0, 0)), row(w), row(w), two, row(d), two, two,
                  pl.BlockSpec((d, LANE), lambda i: (0, 0))],
        out_specs=[tok(d), tok(d), tok(LANE)],
        out_shape=[jax.ShapeDtypeStruct((t, d), F32), jax.ShapeDtypeStruct((t, d), F32),
                   jax.ShapeDtypeStruct((t, LANE), F32)],
        compiler_params=_cparams(("parallel",)),
        name="mix_out",
    )(ya, bonus, g, yb, xa, wo, lnw.reshape(1, w), lnb.reshape(1, w), gt, n2.reshape(1, d), sc, sh, rwt)


def _first_argmax(x, idx, n):
    m = jnp.max(x, axis=0, keepdims=True)
    return m, jnp.min(jnp.where(x == m, idx, n), axis=0, keepdims=True)


def _route_kernel(s_ref, rb_ref, e_out, w_out):
    epg = EXPERTS_PER_GROUP
    tn = s_ref.shape[1]
    idx = lax.broadcasted_iota(jnp.int32, (epg, tn), 0)
    neg = -jnp.inf
    best = None
    for gi in range(N_GROUPS):
        sc = s_ref[gi * epg:(gi + 1) * epg, :]
        bi = sc + rb_ref[gi * epg:(gi + 1) * epg, :]
        m1, i1 = _first_argmax(bi, idx, epg)
        m2 = jnp.max(jnp.where(idx == i1, neg, bi), axis=0, keepdims=True)
        gs = m1 + m2
        if best is None:
            best, g_sel, sc_sel, bi_sel = gs, jnp.zeros((1, tn), jnp.int32), sc, bi
        else:
            better = gs > best
            best = jnp.where(better, gs, best)
            g_sel = jnp.where(better, gi, g_sel)
            sc_sel = jnp.where(better, sc, sc_sel)
            bi_sel = jnp.where(better, bi, bi_sel)
    _, l1 = _first_argmax(bi_sel, idx, epg)
    _, l2 = _first_argmax(jnp.where(idx == l1, neg, bi_sel), idx, epg)
    w1 = jnp.sum(jnp.where(idx == l1, sc_sel, 0.0), axis=0, keepdims=True)
    w2 = jnp.sum(jnp.where(idx == l2, sc_sel, 0.0), axis=0, keepdims=True)
    tot = w1 + w2
    e_out[...] = jnp.concatenate([g_sel * epg + l1, g_sel * epg + l2], axis=0)
    w_out[...] = jnp.concatenate([w1 / tot, w2 / tot], axis=0)


def _route(scores_t, router_b):
    t = scores_t.shape[1]
    tn = _row_tile(t, 1280)
    return pl.pallas_call(
        _route_kernel,
        grid=(t // tn,),
        in_specs=[pl.BlockSpec((N_EXPERTS, tn), lambda i: (0, i)), pl.BlockSpec((N_EXPERTS, 1), lambda i: (0, 0))],
        out_specs=[pl.BlockSpec((TOP_K, tn), lambda i: (0, i)), pl.BlockSpec((TOP_K, tn), lambda i: (0, i))],
        out_shape=[jax.ShapeDtypeStruct((TOP_K, t), jnp.int32), jax.ShapeDtypeStruct((TOP_K, t), F32)],
        compiler_params=_cparams(("parallel",)),
        name="route",
    )(scores_t, router_b.reshape(N_EXPERTS, 1))


def _moe_kernel(be_ref, src_ref, nused_ref, h_hbm, w1_ref, w2_ref, o_hbm, x0, x1, y0, y1, w1b, w2b, gsem, ssem):
    blk = pl.program_id(0)
    n_used = nused_ref[0]
    n_blocks = pl.num_programs(0)
    d = x0.shape[1]
    last_tok = h_hbm.shape[0] - 1
    assert TOP_K == 2

    def gather_copy(b, r, xdst, sem):
        tok = jnp.minimum(jnp.right_shift(src_ref[b * MOE_BLOCK + r], 1), last_tok)
        return pltpu.make_async_copy(h_hbm.at[pl.ds(tok, 1)], xdst.at[pl.ds(r, 1)], sem)

    def wait_rows(xbuf, sem):
        pltpu.make_async_copy(h_hbm.at[pl.ds(0, MOE_BLOCK)], xbuf, sem).wait()

    def wait_results(ybuf, sem):
        pltpu.make_async_copy(ybuf, o_hbm.at[pl.ds(0, MOE_BLOCK), pl.ds(0, d)], sem).wait()

    def scatter_rows(b, ysrc, sem, to_spare):
        for r in range(MOE_BLOCK):
            src = src_ref[b * MOE_BLOCK + r]
            tok = jnp.where(to_spare, last_tok + 1 + r, jnp.right_shift(src, 1))
            lane0 = pl.multiple_of(jnp.where(to_spare, 0, (src & 1) * d), d)
            pltpu.make_async_copy(ysrc.at[pl.ds(r, 1)], o_hbm.at[pl.ds(tok, 1), pl.ds(lane0, d)], sem).start()

    def step(par, xcur, xnext, ycur, yprev):
        first = blk == 0

        @pl.when(first)
        def _():
            yprev[...] = jnp.zeros_like(yprev)
            pltpu.make_async_copy(yprev, o_hbm.at[pl.ds(last_tok + 1, MOE_BLOCK), pl.ds(d, d)], ssem.at[par]).start()

            def body(r, carry):
                gather_copy(blk, r, xcur, gsem.at[par]).start()
                return carry

            lax.fori_loop(0, MOE_BLOCK, body, 0, unroll=8)

        @pl.when(jnp.logical_or(first, be_ref[blk] != be_ref[jnp.maximum(blk - 1, 0)]))
        def _():
            w1b[...] = w1_ref[...].astype(BF16)
            w2b[...] = w2_ref[...].astype(BF16)

        wait_rows(xcur, gsem.at[par])
        nxt = jnp.minimum(blk + 1, n_blocks - 1)
        for r in range(MOE_BLOCK):
            gather_copy(nxt, r, xnext, gsem.at[1 - par]).start()
        scatter_rows(jnp.maximum(blk - 1, 0), yprev, ssem.at[1 - par], first)
        gu = _dot(xcur[...].astype(BF16), w1b[...])
        gate = gu[:, :D_EXPERT]
        act = (gate * jax.nn.sigmoid(gate) * gu[:, D_EXPERT:]).astype(BF16)
        wait_results(ycur, ssem.at[par])
        ycur[...] = _dot(act, w2b[...])

        @pl.when(blk == n_used - 1)
        def _():
            scatter_rows(blk, ycur, ssem.at[par], False)
            wait_results(ycur, ssem.at[par])
            wait_results(yprev, ssem.at[1 - par])
            wait_rows(xnext, gsem.at[1 - par])

    @pl.when(jnp.logical_and(blk < n_used, (blk & 1) == 0))
    def _():
        step(0, x0, x1, y0, y1)

    @pl.when(jnp.logical_and(blk < n_used, (blk & 1) == 1))
    def _():
        step(1, x1, x0, y1, y0)


def _moe_experts(h, block_expert, row_src, n_used, w1, w2, layer):
    t, d = h.shape
    n_blocks = block_expert.shape[0]
    once = pl.Buffered(1)
    grid_spec = pltpu.PrefetchScalarGridSpec(
        num_scalar_prefetch=3,
        grid=(n_blocks,),
        in_specs=[
            pl.BlockSpec(memory_space=pl.ANY),
            pl.BlockSpec((None, None, d, 2 * D_EXPERT), lambda b, be, src, nu: (layer, be[b], 0, 0),
                         pipeline_mode=once),
            pl.BlockSpec((None, None, D_EXPERT, d), lambda b, be, src, nu: (layer, be[b], 0, 0)),
        ],
        out_specs=pl.BlockSpec(memory_space=pl.ANY),
        scratch_shapes=[pltpu.VMEM((MOE_BLOCK, d), F32)] * 4 + [
            pltpu.VMEM((d, 2 * D_EXPERT), BF16), pltpu.VMEM((D_EXPERT, d), BF16),
            pltpu.SemaphoreType.DMA((2,)), pltpu.SemaphoreType.DMA((2,))],
    )
    return pl.pallas_call(
        _moe_kernel,
        grid_spec=grid_spec,
        out_shape=jax.ShapeDtypeStruct((t + MOE_BLOCK, TOP_K * d), F32),
        compiler_params=pltpu.CompilerParams(dimension_semantics=("arbitrary",), vmem_limit_bytes=MOE_VMEM_LIMIT),
        name="moe_experts",
    )(block_expert, row_src, n_used, h, w1, w2)


def _moe_plan(expert):
    t = expert.shape[1]
    m = t * TOP_K
    e_flat = expert.T.reshape(-1)
    onehot = (e_flat[:, None] == jnp.arange(N_EXPERTS, dtype=jnp.int32)[None, :]).astype(jnp.int32)
    csum = jnp.cumsum(onehot, axis=0)
    counts = csum[-1]
    padded = (counts + MOE_BLOCK - 1) // MOE_BLOCK * MOE_BLOCK
    pend = jnp.cumsum(padded)
    pstart = pend - padded
    dest = jnp.sum(onehot * (pstart[None, :] + csum - 1), axis=1)
    m_pad = -(-m // MOE_BLOCK) * MOE_BLOCK + N_EXPERTS * MOE_BLOCK
    n_blocks = m_pad // MOE_BLOCK
    spare = (t + jnp.arange(m_pad, dtype=jnp.int32) % MOE_BLOCK) * TOP_K
    row_src = spare.at[dest].set(jnp.arange(m, dtype=jnp.int32))
    block_start = jnp.arange(n_blocks, dtype=jnp.int32) * MOE_BLOCK
    owner = jnp.sum((block_start[:, None] >= pend[None, :]).astype(jnp.int32), axis=1)
    block_expert = jnp.minimum(owner, N_EXPERTS - 1)
    n_used = (pend[-1] // MOE_BLOCK).astype(jnp.int32).reshape(1)
    return block_expert.astype(jnp.int32), row_src, n_used


def _combine_kernel(o_ref, w_ref, x_ref, gt_ref, fg_ref, out_ref, *, final, tile_off):
    is_ctx = (pl.program_id(0) + tile_off) == 0
    d = x_ref.shape[1]
    wts = w_ref[...]
    f = o_ref[:, 0:d] * wts[:, 0:1] + o_ref[:, d:2 * d] * wts[:, 1:2]
    gt = jnp.where(is_ctx, gt_ref[1:2, :], gt_ref[0:1, :])
    xn = x_ref[...] + gt * f
    if final:
        xn = _rms(xn) * fg_ref[...]
    out_ref[...] = xn


def _combine(o2, wts, xa, gt, final_g, final):
    t, d = xa.shape
    tm = TOKEN_TILE
    off = 1 if final else 0
    nt = t // tm - off
    return pl.pallas_call(
        functools.partial(_combine_kernel, final=final, tile_off=off),
        grid=(nt,),
        in_specs=[pl.BlockSpec((tm, 2 * d), lambda i: (i + off, 0)), pl.BlockSpec((tm, TOP_K), lambda i: (i + off, 0)),
                  pl.BlockSpec((tm, d), lambda i: (i + off, 0)), pl.BlockSpec((2, d), lambda i: (0, 0)),
                  pl.BlockSpec((1, d), lambda i: (0, 0))],
        out_specs=pl.BlockSpec((tm, d), lambda i: (i, 0)),
        out_shape=jax.ShapeDtypeStruct((nt * tm, d), F32),
        compiler_params=_cparams(("parallel",)),
        name="combine",
    )(o2, wts, xa, gt, final_g.reshape(1, d))


def _rope_tables(n):
    pos = jnp.arange(n)
    half = RET_HEAD // 2
    inv = ROPE_BASE ** (-jnp.arange(0, half, 2, dtype=F32) / half)

    def ang(p):
        a = p.astype(F32)[:, None] * inv
        return jnp.concatenate([a, a], axis=-1)

    a = jnp.concatenate([ang(pos // GRID_W), ang(pos % GRID_W)], axis=-1)
    sign = jnp.tile(jnp.concatenate([-jnp.ones((half // 2,), F32), jnp.ones((half // 2,), F32)]), 2)
    cos = jnp.concatenate([jnp.ones((CTX_LEN, RET_HEAD), F32), jnp.cos(a)], axis=0)
    sin = jnp.concatenate([jnp.zeros((CTX_LEN, RET_HEAD), F32), jnp.sin(a) * sign], axis=0)
    return cos, sin


def _pad_rows(w, rows, offset):
    out = jnp.zeros((rows,) + w.shape[1:], w.dtype)
    return lax.dynamic_update_slice_in_dim(out, w, offset, axis=0)


def kernel(x, c, ctx, c_ctx, ada_w, ada_b, norm1_g, norm2_g, w_in, shift_mu, rwkv_w0, rwkv_w2, rwkv_a0, rwkv_a2,
           rwkv_g2, rwkv_k_k, rwkv_k_a, rwkv_r_k, rwkv_lnx_w, rwkv_lnx_b, ret_log2_decay, ret_gn_w, w_out,
           router_w, router_b, moe_w1, moe_w2, final_g):
    b, n, d = x.shape
    assert b == 1 and d == D_MODEL and ctx.shape[1] == CTX_LEN and n % TOKEN_TILE == 0
    depth = ada_w.shape[0]
    w = RWKV_WIDTH

    xa = jnp.concatenate([ctx[0], x[0]], axis=0)
    cct = jnp.stack([c[0], c_ctx], axis=1)
    mod = _ada_mod(cct, ada_w, ada_b).reshape(depth, 2, 6, d)
    cos, sin = _rope_tables(n)
    rwt = jnp.pad(router_w, ((0, 0), (0, LANE - N_EXPERTS)))

    out = None
    for l in range(depth):
        last = l == depth - 1
        sh1, sc1, gt1, sh2, sc2, gt2 = (mod[l, :, i, :] for i in range(6))
        h = _norm_mod(xa, norm1_g[l], sc1, sh1)
        p = _matmul(h, w_in, l, RWKV_PAD, BF16)
        p_ret = _matmul(h, w_in[l:l + 1, :, RWKV_IN:], 0, 4 * RET_WIDTH, BF16)

        mu = jnp.pad(shift_mu[l], ((0, 0), (0, RWKV_PAD - RWKV_IN)))
        w2p = jnp.stack([_pad_rows(rwkv_w2[l, 0], LANE, 0), _pad_rows(rwkv_w2[l, 1], LANE, DECAY_LORA)])
        a2p = jnp.stack([_pad_rows(rwkv_a2[l, 0], LANE, 0), _pad_rows(rwkv_a2[l, 1], LANE, ICLR_LORA)])
        g2p = _pad_rows(rwkv_g2[l], 2 * LANE, 0)
        r, v, kk, g, bonus, lw, kd, bb = _rwkv_prep(
            p, mu, w2p.astype(BF16), a2p.astype(BF16), g2p.astype(BF16), rwkv_w0[l], rwkv_a0[l],
            rwkv_k_k[l].reshape(1, w), rwkv_k_a[l].reshape(1, w), rwkv_r_k[l].reshape(1, w))
        ya = _rwkv_scan(r, v, kk, lw, kd, bb)

        l2d = jnp.broadcast_to(ret_log2_decay[l].astype(F32)[:, :, None], (2, RET_HEADS, LANE))
        yb = _retention(p_ret, cos, sin, l2d, ret_gn_w[l])

        xa, h2, scores_t = _mix_out(ya, bonus, g, yb, xa, w_out[l].astype(BF16), rwkv_lnx_w[l], rwkv_lnx_b[l],
                                    gt1, norm2_g[l], sc2, sh2, rwt)
        expert, gate = _route(scores_t[:, :N_EXPERTS].T, router_b)
        block_expert, row_src, n_used = _moe_plan(expert)
        o2 = _moe_experts(h2, block_expert, row_src, n_used, moe_w1, moe_w2, l)
        res = _combine(o2, gate.T, xa, gt2, final_g, last)
        if last:
            out = res
        else:
            xa = res
    return out[None]
```

```python
import functools
import math

import jax
import jax.numpy as jnp
from jax import lax
from jax.experimental import pallas as pl
from jax.experimental.pallas import tpu as pltpu

F32 = jnp.float32
BF16 = jnp.bfloat16
HIGHEST = lax.Precision.HIGHEST

D_MODEL = 2048
CTX_LEN = 256
GRID_W = 64
NORM_EPS = 1e-6

RWKV_WIDTH = 1024
RWKV_HEAD = 64
DECAY_LORA = 64
ICLR_LORA = 64
GATE_LORA = 160
LNX_EPS = 64e-5
RWKV_IN = 3 * RWKV_WIDTH + 2 * DECAY_LORA + 2 * ICLR_LORA + GATE_LORA
LORA_PAD = 512
RWKV_PAD = 3 * RWKV_WIDTH + LORA_PAD

RET_WIDTH = 1024
RET_HEAD = 128
RET_HEADS = 8
RET_CHUNK = 128
ROPE_BASE = 10000.0

N_EXPERTS = 32
N_GROUPS = 4
EXPERTS_PER_GROUP = 8
TOP_K = 2
D_EXPERT = 1024
MOE_BLOCK = 128

LANE = 128
TOKEN_TILE = 256
RWKV_CHUNK = 64
VMEM_LIMIT = 48 * 1024 * 1024
MOE_VMEM_LIMIT = 56 * 1024 * 1024


def _cparams(sem):
    return pltpu.CompilerParams(dimension_semantics=sem, vmem_limit_bytes=VMEM_LIMIT)


def _dot(a, b, precision=None):
    return jnp.dot(a, b, preferred_element_type=F32, precision=precision)


def _dot_nt(a, b, precision=None):
    return lax.dot_general(a, b, (((1,), (1,)), ((), ())), preferred_element_type=F32, precision=precision)


def _dot_tn(a, b, precision=None):
    return lax.dot_general(a, b, (((0,), (0,)), ((), ())), preferred_element_type=F32, precision=precision)


def _split_bf16(x):
    hi = x.astype(BF16)
    lo = (x - hi.astype(F32)).astype(BF16)
    return hi, lo


def _split3_bf16(x):
    hi = x.astype(BF16)
    rest = x - hi.astype(F32)
    mid = rest.astype(BF16)
    return hi, mid, (rest - mid.astype(F32)).astype(BF16)


def _head_sum(x, head):
    ri = lax.broadcasted_iota(jnp.int32, (LANE, LANE), 0) // head
    ci = lax.broadcasted_iota(jnp.int32, (LANE, LANE), 1) // head
    bd = (ri == ci).astype(BF16)
    outs = []
    for j in range(x.shape[1] // LANE):
        hi, lo = _split_bf16(x[:, j * LANE:(j + 1) * LANE])
        outs.append(_dot(hi, bd) + _dot(lo, bd))
    return outs[0] if len(outs) == 1 else jnp.concatenate(outs, axis=1)


def _ada_kernel(cct_ref, w_ref, b_ref, o_ref):
    k = pl.program_id(2)

    @pl.when(k == 0)
    def _():
        o_ref[0] = jnp.broadcast_to(b_ref[0], o_ref.shape[1:])

    s = cct_ref[...]
    s = s * jax.nn.sigmoid(s)
    w = w_ref[0]
    acc0 = jnp.sum(s[:, 0:1] * w, axis=0, keepdims=True)
    acc1 = jnp.sum(s[:, 1:2] * w, axis=0, keepdims=True)
    o_ref[0] += jnp.concatenate([acc0, acc1], axis=0)


def _ada_mod(cct, ada_w, ada_b):
    depth, d, n6 = ada_w.shape
    tk, tn = 512, 2048
    return pl.pallas_call(
        _ada_kernel,
        grid=(depth, n6 // tn, d // tk),
        in_specs=[
            pl.BlockSpec((tk, 2), lambda l, j, k: (k, 0)),
            pl.BlockSpec((1, tk, tn), lambda l, j, k: (l, k, j)),
            pl.BlockSpec((1, 1, tn), lambda l, j, k: (l, 0, j)),
        ],
        out_specs=pl.BlockSpec((1, 2, tn), lambda l, j, k: (l, 0, j)),
        out_shape=jax.ShapeDtypeStruct((depth, 2, n6), F32),
        compiler_params=_cparams(("parallel", "parallel", "arbitrary")),
        name="ada_mod",
    )(cct, ada_w, ada_b.reshape(depth, 1, n6))


def _rms(x):
    return x * lax.rsqrt(jnp.mean(x * x, axis=-1, keepdims=True) + NORM_EPS)


def _norm_mod_kernel(x_ref, g_ref, sc_ref, sh_ref, o_ref):
    is_ctx = pl.program_id(0) == 0
    y = _rms(x_ref[...]) * g_ref[...]
    sc = jnp.where(is_ctx, sc_ref[1:2, :], sc_ref[0:1, :])
    sh = jnp.where(is_ctx, sh_ref[1:2, :], sh_ref[0:1, :])
    o_ref[...] = (y * (1.0 + sc) + sh).astype(o_ref.dtype)


def _norm_mod(xa, g, sc, sh):
    t, d = xa.shape
    row = pl.BlockSpec((1, d), lambda i: (0, 0))
    two = pl.BlockSpec((2, d), lambda i: (0, 0))
    return pl.pallas_call(
        _norm_mod_kernel,
        grid=(t // TOKEN_TILE,),
        in_specs=[pl.BlockSpec((TOKEN_TILE, d), lambda i: (i, 0)), row, two, two],
        out_specs=pl.BlockSpec((TOKEN_TILE, d), lambda i: (i, 0)),
        out_shape=jax.ShapeDtypeStruct((t, d), BF16),
        compiler_params=_cparams(("parallel",)),
        name="norm_mod",
    )(xa, g.reshape(1, d), sc, sh)


def _matmul_kernel(a_ref, b_ref, o_ref):
    o_ref[...] = _dot(a_ref[...], b_ref[...].astype(BF16)).astype(o_ref.dtype)


def _row_tile(t, cap):
    tm = TOKEN_TILE
    for cand in range(TOKEN_TILE, cap + 1, TOKEN_TILE):
        if t % cand == 0:
            tm = cand
    return tm


def _matmul(a, b, layer, n, out_dtype):
    m, k = a.shape
    tm, tn = _row_tile(m, 1280), 512
    assert n % tn == 0 and n <= b.shape[2]
    return pl.pallas_call(
        _matmul_kernel,
        grid=(m // tm, n // tn),
        in_specs=[pl.BlockSpec((tm, k), lambda i, j: (i, 0)),
                  pl.BlockSpec((None, k, tn), lambda i, j: (layer, 0, j))],
        out_specs=pl.BlockSpec((tm, tn), lambda i, j: (i, j)),
        out_shape=jax.ShapeDtypeStruct((m, n), out_dtype),
        compiler_params=_cparams(("parallel", "arbitrary")),
        name="in_proj",
    )(a, b)


def _rwkv_prep_kernel(pm_ref, pp_ref, pn_ref, mu_ref, w2_ref, a2_ref, g2_ref, w0_ref, a0_ref, kk_ref, ka_ref,
                      rk_ref, r_out, v_out, kk_out, g_out, bonus_out, lw_out, kd_out, b_out):
    i = pl.program_id(0)
    nt = pl.num_programs(0)
    is_ctx = i == 0
    tm = TOKEN_TILE
    row = lax.broadcasted_iota(jnp.int32, (tm, 1), 0)
    in_row = row & (GRID_W - 1)
    edge_lo = jnp.where(is_ctx, row, in_row) == 0
    edge_hi = jnp.where(is_ctx, row - (tm - 1), in_row - (GRID_W - 1)) == 0
    no_up = jnp.logical_or(is_ctx, i == 1)
    no_down = jnp.logical_or(is_ctx, i == nt - 1)
    x_gate = jnp.where(is_ctx, 0.0, 1.0)

    def shifted(c0, c1):
        p = pm_ref[:, c0:c1].astype(F32)
        mu = mu_ref[:, c0:c1]
        prev = jnp.where(edge_lo, 0.0, pltpu.roll(p, 1, axis=0))
        nxt = jnp.where(edge_hi, 0.0, pltpu.roll(p, tm - 1, axis=0))
        out = p + mu[0:1] * (prev - p) + mu[1:2] * (nxt - p)
        above = jnp.where(no_up, 0.0, pp_ref[:, c0:c1].astype(F32))
        below = jnp.where(no_down, 0.0, pn_ref[:, c0:c1].astype(F32))
        up = jnp.concatenate([above, p[:tm - GRID_W]], axis=0)
        down = jnp.concatenate([p[GRID_W:], below], axis=0)
        return out + x_gate * (mu[2:3] * (up - p) + mu[3:4] * (down - p))

    w = RWKV_WIDTH
    ul = shifted(3 * w, 3 * w + LORA_PAD)
    twd = jnp.tanh(ul[:, 0:LANE]).astype(BF16)
    ad = ul[:, LANE:2 * LANE].astype(BF16)
    sg = jax.nn.sigmoid(ul[:, 2 * LANE:4 * LANE]).astype(BF16)
    g_out[...] = _dot(sg, g2_ref[...]).astype(g_out.dtype)

    r = shifted(0, w)
    k = shifted(w, 2 * w)
    v = shifted(2 * w, 3 * w)
    r_out[...] = r.astype(r_out.dtype)
    v_out[...] = v.astype(v_out.dtype)
    kkr = k * kk_ref[...]
    nrm = jnp.sqrt(_head_sum(kkr * kkr, RWKV_HEAD))
    kk = kkr / jnp.maximum(nrm, 1e-12)
    kk_out[...] = kk.astype(kk_out.dtype)
    ksum = jnp.zeros_like(k)
    for d in range(2):
        z = w0_ref[d:d + 1, :] + _dot(twd, w2_ref[d])
        lw_out[d] = -jax.nn.sigmoid(z) * math.exp(-0.5)
        a = jax.nn.sigmoid(a0_ref[d:d + 1, :] + _dot(ad, a2_ref[d]))
        kd = k * (1.0 + (a - 1.0) * ka_ref[...])
        kd_out[d] = kd.astype(kd_out.dtype)
        b_out[d] = (a * kk).astype(b_out.dtype)
        ksum = ksum + kd
    bonus = _head_sum(r * ksum * rk_ref[...], RWKV_HEAD) * v
    bonus_out[...] = bonus.astype(bonus_out.dtype)


def _rwkv_prep(p, mu, w2p, a2p, g2p, w0, a0, k_k, k_a, r_k):
    t = p.shape[0]
    nt = t // TOKEN_TILE
    per_tile = TOKEN_TILE // GRID_W
    n_rows = t // GRID_W
    w = RWKV_WIDTH
    full = lambda shape: pl.BlockSpec(shape, lambda i: (0,) * len(shape))
    tok = pl.BlockSpec((TOKEN_TILE, w), lambda i: (i, 0))
    tok2 = pl.BlockSpec((2, TOKEN_TILE, w), lambda i: (0, i, 0))
    one = jax.ShapeDtypeStruct((t, w), BF16)
    return pl.pallas_call(
        _rwkv_prep_kernel,
        grid=(nt,),
        in_specs=[
            pl.BlockSpec((TOKEN_TILE, RWKV_PAD), lambda i: (i, 0)),
            pl.BlockSpec((GRID_W, RWKV_PAD), lambda i: (jnp.maximum(i * per_tile - 1, 0), 0)),
            pl.BlockSpec((GRID_W, RWKV_PAD), lambda i: (jnp.minimum((i + 1) * per_tile, n_rows - 1), 0)),
            full((4, RWKV_PAD)), full((2, LANE, w)), full((2, LANE, w)), full((2 * LANE, w)),
            full((2, w)), full((2, w)), full((1, w)), full((1, w)), full((1, w)),
        ],
        out_specs=[tok, tok, tok, tok, tok, tok2, tok2, tok2],
        out_shape=[one, one, one, one, one,
                   jax.ShapeDtypeStruct((2, t, w), F32),
                   jax.ShapeDtypeStruct((2, t, w), BF16),
                   jax.ShapeDtypeStruct((2, t, w), BF16)],
        compiler_params=_cparams(("parallel",)),
        name="rwkv_prep",
    )(p, p, p, mu, w2p, a2p, g2p, w0, a0, k_k, k_a, r_k)


RWKV_TILE_PAIRS = 4


def _stack_heads(x):
    lane = lax.broadcasted_iota(jnp.int32, (1, LANE), 1)
    zero = jnp.zeros_like(x)
    return jnp.concatenate([jnp.where(lane < RWKV_HEAD, x, zero), jnp.where(lane >= RWKV_HEAD, x, zero)], axis=0)


def _rwkv_scan_kernel(r_ref, v_ref, kk_ref, lw_ref, kd_ref, b_ref, y_ref, m_ref, lq_s, yl_s, ml_s, el_s):
    d = pl.program_id(0)
    s = pl.program_id(2)
    sign = 1 - 2 * d
    c = RWKV_CHUNK
    tm = TOKEN_TILE
    n_chunks = tm // c
    n_pairs = r_ref.shape[1] // LANE

    @pl.when(s == 0)
    def _():
        m_ref[...] = jnp.zeros_like(m_ref)

    ri = lax.broadcasted_iota(jnp.int32, (tm, tm), 0)
    ci = lax.broadcasted_iota(jnp.int32, (tm, tm), 1)
    same_chunk = (ri // c) == (ci // c)
    along = ((ci - ri) * sign) <= 0
    tri = jnp.where(jnp.logical_and(same_chunk, along), 1.0, 0.0).astype(BF16)
    lw = lw_ref[...]
    lw3 = _split3_bf16(lw)
    lc = sum(_dot(tri, piece) for piece in lw3)
    lend = jnp.concatenate(
        [jnp.broadcast_to(jnp.where(d == 1, lc[cc * c:cc * c + 1], lc[(cc + 1) * c - 1:(cc + 1) * c]),
                          (c, lc.shape[1])) for cc in range(n_chunks)], axis=0)
    e_neg = jnp.exp(-lc)
    e_end = jnp.exp(lend - lc)
    kd = kd_ref[...].astype(F32)
    b = b_ref[...].astype(F32)
    rl_all = r_ref[...].astype(F32) * jnp.exp(lc)
    kke_all = (kk_ref[...].astype(F32) * jnp.exp(lc - lw)).astype(BF16)
    kinv_all = (kd * e_neg).astype(BF16)
    binv_all = (b * e_neg).astype(BF16)
    kdec_all = (kd * e_end).astype(BF16)
    bdec_all = (b * e_end).astype(BF16)

    trow = lax.broadcasted_iota(jnp.int32, (c, LANE), 0)
    tcol = lax.broadcasted_iota(jnp.int32, (c, LANE), 1) & (RWKV_HEAD - 1)
    rel = (tcol - trow) * sign
    incl = rel <= 0
    strict = rel < 0
    eye = jnp.where(rel == 0, 1.0, 0.0)
    hi = lax.broadcasted_iota(jnp.int32, (LANE, LANE), 0) < RWKV_HEAD
    hj = lax.broadcasted_iota(jnp.int32, (LANE, LANE), 1) < RWKV_HEAD
    same_head = hi == hj

    units = [(cc, pr) for cc in range(n_chunks) for pr in range(n_pairs)]

    def sl(x, u):
        cc, pr = u
        return x[cc * c:(cc + 1) * c, pr * LANE:(pr + 1) * LANE]

    rl = [sl(rl_all, u) for u in units]
    kke = [sl(kke_all, u) for u in units]
    vb = [v_ref[u[0] * c:(u[0] + 1) * c, u[1] * LANE:(u[1] + 1) * LANE] for u in units]
    vst = [_stack_heads(x) for x in vb]
    n_units = len(units)
    aa = [_dot_nt(jnp.concatenate([kke[i], rl[i].astype(BF16)], axis=0),
                  jnp.concatenate([_stack_heads(sl(kinv_all, u)), _stack_heads(sl(binv_all, u))], axis=0))
          for i, u in enumerate(units)]
    a_kr = [jnp.concatenate([jnp.where(strict, x[:c, :LANE], 0.0), jnp.where(incl, x[c:, :LANE], 0.0)],
                            axis=0).astype(BF16) for x in aa]
    a_rb = [jnp.where(incl, x[c:, LANE:], 0.0).astype(BF16) for x in aa]
    pw = [jnp.where(strict, -x[:c, LANE:], 0.0) for x in aa]
    tinv = [eye + x for x in pw]
    pwb = [x.astype(BF16) for x in pw]
    pw = [_dot(x, _stack_heads(x)) for x in pwb]
    for _ in range(4):
        pwb = [x.astype(BF16) for x in pw]
        both = [_dot(jnp.concatenate([x, t.astype(BF16)], axis=0), _stack_heads(x)) for x, t in zip(pwb, tinv)]
        pw = [x[:c] for x in both]
        tinv = [t + x[c:] for t, x in zip(tinv, both)]
    tb = [(t + _dot(t.astype(BF16), _stack_heads(x.astype(BF16)))).astype(BF16) for t, x in zip(tinv, pw)]
    avr = [_dot(a, x) for a, x in zip(a_kr, vst)]
    w12 = [_dot(tb[i], jnp.concatenate([_stack_heads(kke[i]), _stack_heads(avr[i][:c].astype(BF16))], axis=1))
           .astype(BF16) for i in range(n_units)]
    corr = [_dot(a_rb[i], jnp.concatenate([_stack_heads(w12[i][:, :LANE]), _stack_heads(w12[i][:, LANE:])], axis=1))
            for i in range(n_units)]
    ones = jnp.ones((3 * c, LANE), BF16)
    for i, u in enumerate(units):
        cc, pr = u
        bw = _dot_tn(sl(bdec_all, u), w12[i])
        lq = jnp.concatenate([rl[i] - corr[i][:, :LANE], jnp.where(same_head, -bw[:, :LANE], 0.0)], axis=0)
        l_hi, l_lo = _split_bf16(lq)
        lq_s[cc, pr] = jnp.concatenate([l_hi, l_lo], axis=1)
        yl_s[cc, pr] = avr[i][c:] - corr[i][:, LANE:]
        ml_s[cc, pr] = jnp.where(same_head, _dot_tn(sl(kdec_all, u), vb[i]) - bw[:, LANE:], 0.0)
        el_s[cc, pr] = jnp.exp(_dot_tn(jnp.concatenate([sl(piece, u) for piece in lw3], axis=0), ones))

    m = [m_ref[pr] for pr in range(n_pairs)]
    for step in range(n_chunks):
        cc = jnp.where(d == 1, n_chunks - 1 - step, step)
        rows = pl.ds(pl.multiple_of(cc * c, c), c)
        for pr in range(n_pairs):
            m_hi, m_lo = _split_bf16(m[pr])
            lq = lq_s[cc, pr]
            res = _dot(lq, jnp.concatenate([m_hi, m_hi], axis=0)) + _dot(lq[:, :LANE], m_lo)
            y_ref[rows, pr * LANE:(pr + 1) * LANE] = (res[:c] + yl_s[cc, pr]).astype(y_ref.dtype)
            m[pr] = el_s[cc, pr] * m[pr] + res[c:] + ml_s[cc, pr]
    for pr in range(n_pairs):
        m_ref[pr] = m[pr]


def _rwkv_scan(r, v, kk, lw, kd, b):
    t, w = r.shape
    nt = t // TOKEN_TILE
    wb = RWKV_TILE_PAIRS * LANE
    n_chunks = TOKEN_TILE // RWKV_CHUNK
    n_pairs = RWKV_TILE_PAIRS

    def tile(d, s):
        return jnp.where(jnp.logical_and(d == 1, s > 0), nt - s, s)

    shared = pl.BlockSpec((TOKEN_TILE, wb), lambda d, j, s: (tile(d, s), j))
    per_dir = pl.BlockSpec((None, TOKEN_TILE, wb), lambda d, j, s: (d, tile(d, s), j))
    unit = lambda rows: pltpu.VMEM((n_chunks, n_pairs, rows, LANE), F32)
    return pl.pallas_call(
        _rwkv_scan_kernel,
        grid=(2, w // wb, nt),
        in_specs=[shared, shared, shared, per_dir, per_dir, per_dir],
        out_specs=per_dir,
        out_shape=jax.ShapeDtypeStruct((2, t, w), BF16),
        scratch_shapes=[pltpu.VMEM((n_pairs, LANE, LANE), F32),
                        pltpu.VMEM((n_chunks, n_pairs, RWKV_CHUNK + LANE, 2 * LANE), BF16),
                        unit(RWKV_CHUNK), unit(LANE), unit(LANE)],
        compiler_params=_cparams(("parallel", "parallel", "arbitrary")),
        name="rwkv_scan",
    )(r, v, kk, lw, kd, b)


def _rotate(t, cos, sin_signed):
    lane = lax.broadcasted_iota(jnp.int32, (1, LANE), 1)
    first = (lane & 63) < 32
    swapped = jnp.where(first, pltpu.roll(t, LANE - 32, axis=1), pltpu.roll(t, 32, axis=1))
    return t * cos + swapped * sin_signed


RET_TILE_HEADS = 2


def _ret_qk(q_ref, k_ref, cos_ref, sin_ref, rows, lanes):
    cos = cos_ref[rows, :]
    sin = sin_ref[rows, :]
    q = _rotate(q_ref[rows, lanes].astype(F32), cos, sin)
    k = _rotate(k_ref[rows, lanes].astype(F32) * (RET_HEAD ** -0.5), cos, sin)
    return q, k


def _log_gamma(l2d_ref, hh):
    lg = jnp.log1p(-jnp.exp2(-l2d_ref[hh:hh + 1, :]))
    return jnp.broadcast_to(lg, (RET_CHUNK, LANE))


def _ret_units(reverse, tile=TOKEN_TILE):
    chunks = range(tile // RET_CHUNK)
    for hh in range(RET_TILE_HEADS):
        for cc in (reversed(chunks) if reverse else chunks):
            yield hh, slice(cc * RET_CHUNK, (cc + 1) * RET_CHUNK), slice(hh * LANE, (hh + 1) * LANE)


def _ret_fwd_kernel(q_ref, k_ref, v_ref, cos_ref, sin_ref, l2d_ref, y_ref, s_ref):
    @pl.when(pl.program_id(1) == 0)
    def _():
        s_ref[...] = jnp.zeros_like(s_ref)

    c = RET_CHUNK
    idx = lax.broadcasted_iota(jnp.int32, (c, LANE), 0).astype(F32)
    for hh, rows, lanes in _ret_units(False, q_ref.shape[0]):
        q, k = _ret_qk(q_ref, k_ref, cos_ref, sin_ref, rows, lanes)
        lg = _log_gamma(l2d_ref, hh)
        s = s_ref[hh]
        y_ref[rows, lanes] = _dot((q * jnp.exp((idx + 1.0) * lg)).astype(BF16), s.astype(BF16)).astype(y_ref.dtype)
        kz = (k * jnp.exp((c - 1.0 - idx) * lg)).astype(BF16)
        s_ref[hh] = s * jnp.exp(c * lg) + _dot_tn(kz, v_ref[rows, lanes])


def _ret_bwd_kernel(q_ref, k_ref, v_ref, g_ref, y1_ref, cos_ref, sin_ref, l2f_ref, l2b_ref, gn_ref, o_ref, s_ref):
    @pl.when(pl.program_id(1) == 0)
    def _():
        s_ref[...] = jnp.zeros_like(s_ref)

    c = RET_CHUNK
    rowi = lax.broadcasted_iota(jnp.int32, (c, LANE), 0)
    coli = lax.broadcasted_iota(jnp.int32, (c, LANE), 1)
    diff = (rowi - coli).astype(F32)
    idx = rowi.astype(F32)
    for hh, rows, lanes in _ret_units(True):
        q, k = _ret_qk(q_ref, k_ref, cos_ref, sin_ref, rows, lanes)
        v = v_ref[rows, lanes]
        lgf = _log_gamma(l2f_ref, hh)
        lgb = _log_gamma(l2b_ref, hh)
        dmat = jnp.where(rowi >= coli, jnp.exp(diff * lgf), jnp.exp(-diff * lgb))
        scores = (_dot_nt(q.astype(BF16), k.astype(BF16)) * dmat).astype(BF16)
        s = s_ref[hh]
        y = (y1_ref[rows, lanes].astype(F32) + _dot(scores, v)
             + _dot((q * jnp.exp((c - idx) * lgb)).astype(BF16), s.astype(BF16)))
        s_ref[hh] = s * jnp.exp(c * lgb) + _dot_tn((k * jnp.exp(idx * lgb)).astype(BF16), v)
        yn = y * lax.rsqrt(jnp.mean(y * y, axis=-1, keepdims=True) + NORM_EPS)
        g = g_ref[rows, lanes].astype(F32)
        o_ref[rows, lanes] = (yn * gn_ref[:, lanes] * (g * jax.nn.sigmoid(g))).astype(o_ref.dtype)


def _retention(p, cos, sin, l2d, gn_w):
    t = p.shape[0]
    nt = t // TOKEN_TILE
    wb = RET_TILE_HEADS * LANE
    n_col = RET_WIDTH // wb
    base = 0
    l2d = l2d.reshape(2, n_col, RET_TILE_HEADS, LANE)

    def rtile(s):
        return jnp.where(s > 0, nt - s, s)

    blk = (TOKEN_TILE, wb)
    rope = (TOKEN_TILE, LANE)
    dec = lambda d: pl.BlockSpec((None, None, RET_TILE_HEADS, LANE), lambda j, s: (d, j, 0, 0))
    rcol = lambda part: pl.BlockSpec(blk, lambda j, s: (rtile(s), base + part * n_col + j))
    scratch = [pltpu.VMEM((RET_TILE_HEADS, RET_HEAD, RET_HEAD), F32)]
    tf = _row_tile(t, 1280)
    fcol = lambda part: pl.BlockSpec((tf, wb), lambda j, s: (s, base + part * n_col + j))
    y1 = pl.pallas_call(
        _ret_fwd_kernel,
        grid=(n_col, t // tf),
        in_specs=[fcol(0), fcol(1), fcol(2), pl.BlockSpec((tf, LANE), lambda j, s: (s, 0)),
                  pl.BlockSpec((tf, LANE), lambda j, s: (s, 0)), dec(0)],
        out_specs=pl.BlockSpec((tf, wb), lambda j, s: (s, j)),
        out_shape=jax.ShapeDtypeStruct((t, RET_WIDTH), BF16),
        scratch_shapes=scratch,
        compiler_params=_cparams(("parallel", "arbitrary")),
        name="ret_fwd",
    )(p, p, p, cos, sin, l2d)
    return pl.pallas_call(
        _ret_bwd_kernel,
        grid=(n_col, nt),
        in_specs=[rcol(0), rcol(1), rcol(2), rcol(3), pl.BlockSpec(blk, lambda j, s: (rtile(s), j)),
                  pl.BlockSpec(rope, lambda j, s: (rtile(s), 0)), pl.BlockSpec(rope, lambda j, s: (rtile(s), 0)),
                  dec(0), dec(1), pl.BlockSpec((1, wb), lambda j, s: (0, j))],
        out_specs=pl.BlockSpec(blk, lambda j, s: (rtile(s), j)),
        out_shape=jax.ShapeDtypeStruct((t, RET_WIDTH), BF16),
        scratch_shapes=scratch,
        compiler_params=_cparams(("parallel", "arbitrary")),
        name="ret_bwd",
    )(p, p, p, p, y1, cos, sin, l2d, l2d, gn_w.reshape(1, RET_WIDTH))


def _mix_out_kernel(ya_ref, bonus_ref, g_ref, yb_ref, x_ref, wo_ref, lnw_ref, lnb_ref, gt_ref, n2_ref, sc_ref,
                    sh_ref, rw_ref, x_out, h_out, s_out):
    is_ctx = pl.program_id(0) == 0
    y = ya_ref[0].astype(F32) + ya_ref[1].astype(F32)
    inv_n = 1.0 / RWKV_HEAD
    mu = _head_sum(y, RWKV_HEAD) * inv_n
    yc = y - mu
    var = _head_sum(yc * yc, RWKV_HEAD) * inv_n
    yn = yc * lax.rsqrt(var + LNX_EPS)
    ya = (yn * lnw_ref[...] + lnb_ref[...] + bonus_ref[...].astype(F32)) * g_ref[...].astype(F32)
    z = _dot(ya.astype(BF16), wo_ref[0:RWKV_WIDTH, :]) + _dot(yb_ref[...], wo_ref[RWKV_WIDTH:, :])
    pick = lambda ref: jnp.where(is_ctx, ref[1:2, :], ref[0:1, :])
    xn = x_ref[...] + pick(gt_ref) * z
    x_out[...] = xn
    h = _rms(xn) * n2_ref[...] * (1.0 + pick(sc_ref)) + pick(sh_ref)
    h_out[...] = h
    h_hi, h_lo = _split_bf16(h)
    r_hi, r_lo = _split_bf16(rw_ref[...])
    s_out[...] = jax.nn.sigmoid(_dot(h_hi, r_hi) + _dot(h_hi, r_lo) + _dot(h_lo, r_hi))


def _mix_out(ya, bonus, g, yb, xa, wo, lnw, lnb, gt, n2, sc, sh, rwt):
    t, d = xa.shape
    w = RWKV_WIDTH
    tm = TOKEN_TILE
    row = lambda n: pl.BlockSpec((1, n), lambda i: (0, 0))
    two = pl.BlockSpec((2, d), lambda i: (0, 0))
    tok = lambda n: pl.BlockSpec((tm, n), lambda i: (i, 0))
    return pl.pallas_call(
        _mix_out_kernel,
        grid=(t // tm,),
        in_specs=[pl.BlockSpec((2, tm, w), lambda i: (0, i, 0)), tok(w), tok(w), tok(w), tok(d),
                  pl.BlockSpec((d, d), lambda i: (0, 0)), row(w), row(w), two, row(d), two, two,
                  pl.BlockSpec((d, LANE), lambda i: (0, 0))],
        out_specs=[tok(d), tok(d), tok(LANE)],
        out_shape=[jax.ShapeDtypeStruct((t, d), F32), jax.ShapeDtypeStruct((t, d), F32),
                   jax.ShapeDtypeStruct((t, LANE), F32)],
        compiler_params=_cparams(("parallel",)),
        name="mix_out",
    )(ya, bonus, g, yb, xa, wo, lnw.reshape(1, w), lnb.reshape(1, w), gt, n2.reshape(1, d), sc, sh, rwt)


def _first_argmax(x, idx, n):
    m = jnp.max(x, axis=0, keepdims=True)
    return m, jnp.min(jnp.where(x == m, idx, n), axis=0, keepdims=True)


def _route_kernel(s_ref, rb_ref, e_out, w_out):
    epg = EXPERTS_PER_GROUP
    tn = s_ref.shape[1]
    idx = lax.broadcasted_iota(jnp.int32, (epg, tn), 0)
    neg = -jnp.inf
    best = None
    for gi in range(N_GROUPS):
        sc = s_ref[gi * epg:(gi + 1) * epg, :]
        bi = sc + rb_ref[gi * epg:(gi + 1) * epg, :]
        m1, i1 = _first_argmax(bi, idx, epg)
        m2 = jnp.max(jnp.where(idx == i1, neg, bi), axis=0, keepdims=True)
        gs = m1 + m2
        if best is None:
            best, g_sel, sc_sel, bi_sel = gs, jnp.zeros((1, tn), jnp.int32), sc, bi
        else:
            better = gs > best
            best = jnp.where(better, gs, best)
            g_sel = jnp.where(better, gi, g_sel)
            sc_sel = jnp.where(better, sc, sc_sel)
            bi_sel = jnp.where(better, bi, bi_sel)
    _, l1 = _first_argmax(bi_sel, idx, epg)
    _, l2 = _first_argmax(jnp.where(idx == l1, neg, bi_sel), idx, epg)
    w1 = jnp.sum(jnp.where(idx == l1, sc_sel, 0.0), axis=0, keepdims=True)
    w2 = jnp.sum(jnp.where(idx == l2, sc_sel, 0.0), axis=0, keepdims=True)
    tot = w1 + w2
    e_out[...] = jnp.concatenate([g_sel * epg + l1, g_sel * epg + l2], axis=0)
    w_out[...] = jnp.concatenate([w1 / tot, w2 / tot], axis=0)


def _route(scores_t, router_b):
    t = scores_t.shape[1]
    tn = _row_tile(t, 1280)
    return pl.pallas_call(
        _route_kernel,
        grid=(t // tn,),
        in_specs=[pl.BlockSpec((N_EXPERTS, tn), lambda i: (0, i)), pl.BlockSpec((N_EXPERTS, 1), lambda i: (0, 0))],
        out_specs=[pl.BlockSpec((TOP_K, tn), lambda i: (0, i)), pl.BlockSpec((TOP_K, tn), lambda i: (0, i))],
        out_shape=[jax.ShapeDtypeStruct((TOP_K, t), jnp.int32), jax.ShapeDtypeStruct((TOP_K, t), F32)],
        compiler_params=_cparams(("parallel",)),
        name="route",
    )(scores_t, router_b.reshape(N_EXPERTS, 1))


def _moe_kernel(be_ref, src_ref, nused_ref, h_hbm, w1_ref, w2_ref, o_hbm, x0, x1, y0, y1, w1b, w2b, gsem, ssem):
    blk = pl.program_id(0)
    n_used = nused_ref[0]
    n_blocks = pl.num_programs(0)
    d = x0.shape[1]
    last_tok = h_hbm.shape[0] - 1
    assert TOP_K == 2

    def gather_copy(b, r, xdst, sem):
        tok = jnp.minimum(jnp.right_shift(src_ref[b * MOE_BLOCK + r], 1), last_tok)
        return pltpu.make_async_copy(h_hbm.at[pl.ds(tok, 1)], xdst.at[pl.ds(r, 1)], sem)

    def wait_rows(xbuf, sem):
        pltpu.make_async_copy(h_hbm.at[pl.ds(0, MOE_BLOCK)], xbuf, sem).wait()

    def wait_results(ybuf, sem):
        pltpu.make_async_copy(ybuf, o_hbm.at[pl.ds(0, MOE_BLOCK), pl.ds(0, d)], sem).wait()

    def scatter_rows(b, ysrc, sem, to_spare):
        for r in range(MOE_BLOCK):
            src = src_ref[b * MOE_BLOCK + r]
            tok = jnp.where(to_spare, last_tok + 1 + r, jnp.right_shift(src, 1))
            lane0 = pl.multiple_of(jnp.where(to_spare, 0, (src & 1) * d), d)
            pltpu.make_async_copy(ysrc.at[pl.ds(r, 1)], o_hbm.at[pl.ds(tok, 1), pl.ds(lane0, d)], sem).start()

    def step(par, xcur, xnext, ycur, yprev):
        first = blk == 0

        @pl.when(first)
        def _():
            yprev[...] = jnp.zeros_like(yprev)
            pltpu.make_async_copy(yprev, o_hbm.at[pl.ds(last_tok + 1, MOE_BLOCK), pl.ds(d, d)], ssem.at[par]).start()

            def body(r, carry):
                gather_copy(blk, r, xcur, gsem.at[par]).start()
                return carry

            lax.fori_loop(0, MOE_BLOCK, body, 0, unroll=8)

        @pl.when(jnp.logical_or(first, be_ref[blk] != be_ref[jnp.maximum(blk - 1, 0)]))
        def _():
            w1b[...] = w1_ref[...].astype(BF16)
            w2b[...] = w2_ref[...].astype(BF16)

        wait_rows(xcur, gsem.at[par])
        nxt = jnp.minimum(blk + 1, n_blocks - 1)
        for r in range(MOE_BLOCK):
            gather_copy(nxt, r, xnext, gsem.at[1 - par]).start()
        scatter_rows(jnp.maximum(blk - 1, 0), yprev, ssem.at[1 - par], first)
        gu = _dot(xcur[...].astype(BF16), w1b[...])
        gate = gu[:, :D_EXPERT]
        act = (gate * jax.nn.sigmoid(gate) * gu[:, D_EXPERT:]).astype(BF16)
        wait_results(ycur, ssem.at[par])
        ycur[...] = _dot(act, w2b[...])

        @pl.when(blk == n_used - 1)
        def _():
            scatter_rows(blk, ycur, ssem.at[par], False)
            wait_results(ycur, ssem.at[par])
            wait_results(yprev, ssem.at[1 - par])
            wait_rows(xnext, gsem.at[1 - par])

    @pl.when(jnp.logical_and(blk < n_used, (blk & 1) == 0))
    def _():
        step(0, x0, x1, y0, y1)

    @pl.when(jnp.logical_and(blk < n_used, (blk & 1) == 1))
    def _():
        step(1, x1, x0, y1, y0)


def _moe_experts(h, block_expert, row_src, n_used, w1, w2, layer):
    t, d = h.shape
    n_blocks = block_expert.shape[0]
    once = pl.Buffered(1)
    grid_spec = pltpu.PrefetchScalarGridSpec(
        num_scalar_prefetch=3,
        grid=(n_blocks,),
        in_specs=[
            pl.BlockSpec(memory_space=pl.ANY),
            pl.BlockSpec((None, None, d, 2 * D_EXPERT), lambda b, be, src, nu: (layer, be[b], 0, 0),
                         pipeline_mode=once),
            pl.BlockSpec((None, None, D_EXPERT, d), lambda b, be, src, nu: (layer, be[b], 0, 0)),
        ],
        out_specs=pl.BlockSpec(memory_space=pl.ANY),
        scratch_shapes=[pltpu.VMEM((MOE_BLOCK, d), F32)] * 4 + [
            pltpu.VMEM((d, 2 * D_EXPERT), BF16), pltpu.VMEM((D_EXPERT, d), BF16),
            pltpu.SemaphoreType.DMA((2,)), pltpu.SemaphoreType.DMA((2,))],
    )
    return pl.pallas_call(
        _moe_kernel,
        grid_spec=grid_spec,
        out_shape=jax.ShapeDtypeStruct((t + MOE_BLOCK, TOP_K * d), F32),
        compiler_params=pltpu.CompilerParams(dimension_semantics=("arbitrary",), vmem_limit_bytes=MOE_VMEM_LIMIT),
        name="moe_experts",
    )(block_expert, row_src, n_used, h, w1, w2)


def _moe_plan(expert):
    t = expert.shape[1]
    m = t * TOP_K
    e_flat = expert.T.reshape(-1)
    onehot = (e_flat[:, None] == jnp.arange(N_EXPERTS, dtype=jnp.int32)[None, :]).astype(jnp.int32)
    csum = jnp.cumsum(onehot, axis=0)
    counts = csum[-1]
    padded = (counts + MOE_BLOCK - 1) // MOE_BLOCK * MOE_BLOCK
    pend = jnp.cumsum(padded)
    pstart = pend - padded
    dest = jnp.sum(onehot * (pstart[None, :] + csum - 1), axis=1)
    m_pad = -(-m // MOE_BLOCK) * MOE_BLOCK + N_EXPERTS * MOE_BLOCK
    n_blocks = m_pad // MOE_BLOCK
    spare = (t + jnp.arange(m_pad, dtype=jnp.int32) % MOE_BLOCK) * TOP_K
    row_src = spare.at[dest].set(jnp.arange(m, dtype=jnp.int32), unique_indices=True, mode='promise_in_bounds')
    block_start = jnp.arange(n_blocks, dtype=jnp.int32) * MOE_BLOCK
    owner = jnp.sum((block_start[:, None] >= pend[None, :]).astype(jnp.int32), axis=1)
    block_expert = jnp.minimum(owner, N_EXPERTS - 1)
    n_used = (pend[-1] // MOE_BLOCK).astype(jnp.int32).reshape(1)
    return block_expert.astype(jnp.int32), row_src, n_used


def _combine_kernel(o_ref, w_ref, x_ref, gt_ref, fg_ref, out_ref, *, final, tile_off):
    is_ctx = (pl.program_id(0) + tile_off) == 0
    d = x_ref.shape[1]
    wts = w_ref[...]
    f = o_ref[:, 0:d] * wts[:, 0:1] + o_ref[:, d:2 * d] * wts[:, 1:2]
    gt = jnp.where(is_ctx, gt_ref[1:2, :], gt_ref[0:1, :])
    xn = x_ref[...] + gt * f
    if final:
        xn = _rms(xn) * fg_ref[...]
    out_ref[...] = xn


def _combine(o2, wts, xa, gt, final_g, final):
    t, d = xa.shape
    tm = TOKEN_TILE
    off = 1 if final else 0
    nt = t // tm - off
    return pl.pallas_call(
        functools.partial(_combine_kernel, final=final, tile_off=off),
        grid=(nt,),
        in_specs=[pl.BlockSpec((tm, 2 * d), lambda i: (i + off, 0)), pl.BlockSpec((tm, TOP_K), lambda i: (i + off, 0)),
                  pl.BlockSpec((tm, d), lambda i: (i + off, 0)), pl.BlockSpec((2, d), lambda i: (0, 0)),
                  pl.BlockSpec((1, d), lambda i: (0, 0))],
        out_specs=pl.BlockSpec((tm, d), lambda i: (i, 0)),
        out_shape=jax.ShapeDtypeStruct((nt * tm, d), F32),
        compiler_params=_cparams(("parallel",)),
        name="combine",
    )(o2, wts, xa, gt, final_g.reshape(1, d))


def _rope_tables(n):
    pos = jnp.arange(n)
    half = RET_HEAD // 2
    inv = ROPE_BASE ** (-jnp.arange(0, half, 2, dtype=F32) / half)

    def ang(p):
        a = p.astype(F32)[:, None] * inv
        return jnp.concatenate([a, a], axis=-1)

    a = jnp.concatenate([ang(pos // GRID_W), ang(pos % GRID_W)], axis=-1)
    sign = jnp.tile(jnp.concatenate([-jnp.ones((half // 2,), F32), jnp.ones((half // 2,), F32)]), 2)
    cos = jnp.concatenate([jnp.ones((CTX_LEN, RET_HEAD), F32), jnp.cos(a)], axis=0)
    sin = jnp.concatenate([jnp.zeros((CTX_LEN, RET_HEAD), F32), jnp.sin(a) * sign], axis=0)
    return cos, sin


def _pad_rows(w, rows, offset):
    out = jnp.zeros((rows,) + w.shape[1:], w.dtype)
    return lax.dynamic_update_slice_in_dim(out, w, offset, axis=0)


def kernel(x, c, ctx, c_ctx, ada_w, ada_b, norm1_g, norm2_g, w_in, shift_mu, rwkv_w0, rwkv_w2, rwkv_a0, rwkv_a2,
           rwkv_g2, rwkv_k_k, rwkv_k_a, rwkv_r_k, rwkv_lnx_w, rwkv_lnx_b, ret_log2_decay, ret_gn_w, w_out,
           router_w, router_b, moe_w1, moe_w2, final_g):
    b, n, d = x.shape
    assert b == 1 and d == D_MODEL and ctx.shape[1] == CTX_LEN and n % TOKEN_TILE == 0
    depth = ada_w.shape[0]
    w = RWKV_WIDTH

    xa = jnp.concatenate([ctx[0], x[0]], axis=0)
    cct = jnp.stack([c[0], c_ctx], axis=1)
    mod = _ada_mod(cct, ada_w, ada_b).reshape(depth, 2, 6, d)
    cos, sin = _rope_tables(n)
    rwt = jnp.pad(router_w, ((0, 0), (0, LANE - N_EXPERTS)))

    out = None
    for l in range(depth):
        last = l == depth - 1
        sh1, sc1, gt1, sh2, sc2, gt2 = (mod[l, :, i, :] for i in range(6))
        h = _norm_mod(xa, norm1_g[l], sc1, sh1)
        p = _matmul(h, w_in, l, RWKV_PAD, BF16)
        p_ret = _matmul(h, w_in[l:l + 1, :, RWKV_IN:], 0, 4 * RET_WIDTH, BF16)

        mu = jnp.pad(shift_mu[l], ((0, 0), (0, RWKV_PAD - RWKV_IN)))
        w2p = jnp.stack([_pad_rows(rwkv_w2[l, 0], LANE, 0), _pad_rows(rwkv_w2[l, 1], LANE, DECAY_LORA)])
        a2p = jnp.stack([_pad_rows(rwkv_a2[l, 0], LANE, 0), _pad_rows(rwkv_a2[l, 1], LANE, ICLR_LORA)])
        g2p = _pad_rows(rwkv_g2[l], 2 * LANE, 0)
        r, v, kk, g, bonus, lw, kd, bb = _rwkv_prep(
            p, mu, w2p.astype(BF16), a2p.astype(BF16), g2p.astype(BF16), rwkv_w0[l], rwkv_a0[l],
            rwkv_k_k[l].reshape(1, w), rwkv_k_a[l].reshape(1, w), rwkv_r_k[l].reshape(1, w))
        ya = _rwkv_scan(r, v, kk, lw, kd, bb)

        l2d = jnp.broadcast_to(ret_log2_decay[l].astype(F32)[:, :, None], (2, RET_HEADS, LANE))
        yb = _retention(p_ret, cos, sin, l2d, ret_gn_w[l])

        xa, h2, scores_t = _mix_out(ya, bonus, g, yb, xa, w_out[l].astype(BF16), rwkv_lnx_w[l], rwkv_lnx_b[l],
                                    gt1, norm2_g[l], sc2, sh2, rwt)
        expert, gate = _route(scores_t[:, :N_EXPERTS].T, router_b)
        block_expert, row_src, n_used = _moe_plan(expert)
        o2 = _moe_experts(h2, block_expert, row_src, n_used, moe_w1, moe_w2, l)
        res = _combine(o2, gate.T, xa, gt2, final_g, last)
        if last:
            out = res
        else:
            xa = res
    return out[None]
```

```python
import functools
import math

import jax
import jax.numpy as jnp
from jax import lax
from jax.experimental import pallas as pl
from jax.experimental.pallas import tpu as pltpu

F32 = jnp.float32
BF16 = jnp.bfloat16
HIGHEST = lax.Precision.HIGHEST

D_MODEL = 2048
CTX_LEN = 256
GRID_W = 64
NORM_EPS = 1e-6

RWKV_WIDTH = 1024
RWKV_HEAD = 64
DECAY_LORA = 64
ICLR_LORA = 64
GATE_LORA = 160
LNX_EPS = 64e-5
RWKV_IN = 3 * RWKV_WIDTH + 2 * DECAY_LORA + 2 * ICLR_LORA + GATE_LORA
LORA_PAD = 512
RWKV_PAD = 3 * RWKV_WIDTH + LORA_PAD

RET_WIDTH = 1024
RET_HEAD = 128
RET_HEADS = 8
RET_CHUNK = 128
ROPE_BASE = 10000.0

N_EXPERTS = 32
N_GROUPS = 4
EXPERTS_PER_GROUP = 8
TOP_K = 2
D_EXPERT = 1024
MOE_BLOCK = 128

LANE = 128
TOKEN_TILE = 256
RWKV_CHUNK = 64
VMEM_LIMIT = 48 * 1024 * 1024
MOE_VMEM_LIMIT = 56 * 1024 * 1024


def _cparams(sem):
    return pltpu.CompilerParams(dimension_semantics=sem, vmem_limit_bytes=VMEM_LIMIT)


def _dot(a, b, precision=None):
    return jnp.dot(a, b, preferred_element_type=F32, precision=precision)


def _dot_nt(a, b, precision=None):
    return lax.dot_general(a, b, (((1,), (1,)), ((), ())), preferred_element_type=F32, precision=precision)


def _dot_tn(a, b, precision=None):
    return lax.dot_general(a, b, (((0,), (0,)), ((), ())), preferred_element_type=F32, precision=precision)


def _split_bf16(x):
    hi = x.astype(BF16)
    lo = (x - hi.astype(F32)).astype(BF16)
    return hi, lo


def _split3_bf16(x):
    hi = x.astype(BF16)
    rest = x - hi.astype(F32)
    mid = rest.astype(BF16)
    return hi, mid, (rest - mid.astype(F32)).astype(BF16)


def _head_sum(x, head):
    ri = lax.broadcasted_iota(jnp.int32, (LANE, LANE), 0) // head
    ci = lax.broadcasted_iota(jnp.int32, (LANE, LANE), 1) // head
    bd = (ri == ci).astype(BF16)
    outs = []
    for j in range(x.shape[1] // LANE):
        hi, lo = _split_bf16(x[:, j * LANE:(j + 1) * LANE])
        outs.append(_dot(hi, bd) + _dot(lo, bd))
    return outs[0] if len(outs) == 1 else jnp.concatenate(outs, axis=1)


def _ada_kernel(cct_ref, w_ref, b_ref, o_ref):
    k = pl.program_id(2)

    @pl.when(k == 0)
    def _():
        o_ref[0] = jnp.broadcast_to(b_ref[0], o_ref.shape[1:])

    s = cct_ref[...]
    s = s * jax.nn.sigmoid(s)
    w = w_ref[0]
    acc0 = jnp.sum(s[:, 0:1] * w, axis=0, keepdims=True)
    acc1 = jnp.sum(s[:, 1:2] * w, axis=0, keepdims=True)
    o_ref[0] += jnp.concatenate([acc0, acc1], axis=0)


def _ada_mod(cct, ada_w, ada_b):
    depth, d, n6 = ada_w.shape
    tk, tn = 512, 2048
    return pl.pallas_call(
        _ada_kernel,
        grid=(depth, n6 // tn, d // tk),
        in_specs=[
            pl.BlockSpec((tk, 2), lambda l, j, k: (k, 0)),
            pl.BlockSpec((1, tk, tn), lambda l, j, k: (l, k, j)),
            pl.BlockSpec((1, 1, tn), lambda l, j, k: (l, 0, j)),
        ],
        out_specs=pl.BlockSpec((1, 2, tn), lambda l, j, k: (l, 0, j)),
        out_shape=jax.ShapeDtypeStruct((depth, 2, n6), F32),
        compiler_params=_cparams(("parallel", "parallel", "arbitrary")),
        name="ada_mod",
    )(cct, ada_w, ada_b.reshape(depth, 1, n6))


def _rms(x):
    return x * lax.rsqrt(jnp.mean(x * x, axis=-1, keepdims=True) + NORM_EPS)


def _norm_mod_kernel(x_ref, g_ref, sc_ref, sh_ref, o_ref):
    is_ctx = pl.program_id(0) == 0
    y = _rms(x_ref[...]) * g_ref[...]
    sc = jnp.where(is_ctx, sc_ref[1:2, :], sc_ref[0:1, :])
    sh = jnp.where(is_ctx, sh_ref[1:2, :], sh_ref[0:1, :])
    o_ref[...] = (y * (1.0 + sc) + sh).astype(o_ref.dtype)


def _norm_mod(xa, g, sc, sh):
    t, d = xa.shape
    row = pl.BlockSpec((1, d), lambda i: (0, 0))
    two = pl.BlockSpec((2, d), lambda i: (0, 0))
    return pl.pallas_call(
        _norm_mod_kernel,
        grid=(t // TOKEN_TILE,),
        in_specs=[pl.BlockSpec((TOKEN_TILE, d), lambda i: (i, 0)), row, two, two],
        out_specs=pl.BlockSpec((TOKEN_TILE, d), lambda i: (i, 0)),
        out_shape=jax.ShapeDtypeStruct((t, d), BF16),
        compiler_params=_cparams(("parallel",)),
        name="norm_mod",
    )(xa, g.reshape(1, d), sc, sh)


def _matmul_kernel(a_ref, b_ref, o_ref):
    o_ref[...] = _dot(a_ref[...], b_ref[...].astype(BF16)).astype(o_ref.dtype)


def _row_tile(t, cap):
    tm = TOKEN_TILE
    for cand in range(TOKEN_TILE, cap + 1, TOKEN_TILE):
        if t % cand == 0:
            tm = cand
    return tm


def _matmul(a, b, layer, n, out_dtype):
    m, k = a.shape
    tm, tn = _row_tile(m, 1280), 512
    assert n % tn == 0 and n <= b.shape[2]
    return pl.pallas_call(
        _matmul_kernel,
        grid=(m // tm, n // tn),
        in_specs=[pl.BlockSpec((tm, k), lambda i, j: (i, 0)),
                  pl.BlockSpec((None, k, tn), lambda i, j: (layer, 0, j))],
        out_specs=pl.BlockSpec((tm, tn), lambda i, j: (i, j)),
        out_shape=jax.ShapeDtypeStruct((m, n), out_dtype),
        compiler_params=_cparams(("parallel", "arbitrary")),
        name="in_proj",
    )(a, b)


def _rwkv_prep_kernel(pm_ref, pp_ref, pn_ref, mu_ref, w2_ref, a2_ref, g2_ref, w0_ref, a0_ref, kk_ref, ka_ref,
                      rk_ref, r_out, v_out, kk_out, g_out, bonus_out, lw_out, kd_out, b_out):
    i = pl.program_id(0)
    nt = pl.num_programs(0)
    is_ctx = i == 0
    tm = TOKEN_TILE
    row = lax.broadcasted_iota(jnp.int32, (tm, 1), 0)
    in_row = row & (GRID_W - 1)
    edge_lo = jnp.where(is_ctx, row, in_row) == 0
    edge_hi = jnp.where(is_ctx, row - (tm - 1), in_row - (GRID_W - 1)) == 0
    no_up = jnp.logical_or(is_ctx, i == 1)
    no_down = jnp.logical_or(is_ctx, i == nt - 1)
    x_gate = jnp.where(is_ctx, 0.0, 1.0)

    ri = lax.broadcasted_iota(jnp.int32, (tm, tm), 0)
    ci = lax.broadcasted_iota(jnp.int32, (tm, tm), 1)
    sel_prev = jnp.where(ci == ri - 1, jnp.where(edge_lo, 0.0, 1.0), 0.0).astype(BF16)
    sel_next = jnp.where(ci == ri + 1, jnp.where(edge_hi, 0.0, 1.0), 0.0).astype(BF16)

    def shifted(c0, c1):
        pb = pm_ref[:, c0:c1]
        p = pb.astype(F32)
        mu = mu_ref[:, c0:c1]
        prev = _dot(sel_prev, pb)
        nxt = _dot(sel_next, pb)
        out = p + mu[0:1] * (prev - p) + mu[1:2] * (nxt - p)
        above = jnp.where(no_up, 0.0, pp_ref[:, c0:c1].astype(F32))
        below = jnp.where(no_down, 0.0, pn_ref[:, c0:c1].astype(F32))
        up = jnp.concatenate([above, p[:tm - GRID_W]], axis=0)
        down = jnp.concatenate([p[GRID_W:], below], axis=0)
        return out + x_gate * (mu[2:3] * (up - p) + mu[3:4] * (down - p))

    w = RWKV_WIDTH
    ul = shifted(3 * w, 3 * w + LORA_PAD)
    twd = jnp.tanh(ul[:, 0:LANE]).astype(BF16)
    ad = ul[:, LANE:2 * LANE].astype(BF16)
    sg = jax.nn.sigmoid(ul[:, 2 * LANE:4 * LANE]).astype(BF16)
    g_out[...] = _dot(sg, g2_ref[...]).astype(g_out.dtype)

    r = shifted(0, w)
    k = shifted(w, 2 * w)
    v = shifted(2 * w, 3 * w)
    r_out[...] = r.astype(r_out.dtype)
    v_out[...] = v.astype(v_out.dtype)
    kkr = k * kk_ref[...]
    nrm = jnp.sqrt(_head_sum(kkr * kkr, RWKV_HEAD))
    kk = kkr / jnp.maximum(nrm, 1e-12)
    kk_out[...] = kk.astype(kk_out.dtype)
    ksum = jnp.zeros_like(k)
    for d in range(2):
        z = w0_ref[d:d + 1, :] + _dot(twd, w2_ref[d])
        lw_out[d] = -jax.nn.sigmoid(z) * math.exp(-0.5)
        a = jax.nn.sigmoid(a0_ref[d:d + 1, :] + _dot(ad, a2_ref[d]))
        kd = k * (1.0 + (a - 1.0) * ka_ref[...])
        kd_out[d] = kd.astype(kd_out.dtype)
        b_out[d] = (a * kk).astype(b_out.dtype)
        ksum = ksum + kd
    bonus = _head_sum(r * ksum * rk_ref[...], RWKV_HEAD) * v
    bonus_out[...] = bonus.astype(bonus_out.dtype)


def _rwkv_prep(p, mu, w2p, a2p, g2p, w0, a0, k_k, k_a, r_k):
    t = p.shape[0]
    nt = t // TOKEN_TILE
    per_tile = TOKEN_TILE // GRID_W
    n_rows = t // GRID_W
    w = RWKV_WIDTH
    full = lambda shape: pl.BlockSpec(shape, lambda i: (0,) * len(shape))
    tok = pl.BlockSpec((TOKEN_TILE, w), lambda i: (i, 0))
    tok2 = pl.BlockSpec((2, TOKEN_TILE, w), lambda i: (0, i, 0))
    one = jax.ShapeDtypeStruct((t, w), BF16)
    return pl.pallas_call(
        _rwkv_prep_kernel,
        grid=(nt,),
        in_specs=[
            pl.BlockSpec((TOKEN_TILE, RWKV_PAD), lambda i: (i, 0)),
            pl.BlockSpec((GRID_W, RWKV_PAD), lambda i: (jnp.maximum(i * per_tile - 1, 0), 0)),
            pl.BlockSpec((GRID_W, RWKV_PAD), lambda i: (jnp.minimum((i + 1) * per_tile, n_rows - 1), 0)),
            full((4, RWKV_PAD)), full((2, LANE, w)), full((2, LANE, w)), full((2 * LANE, w)),
            full((2, w)), full((2, w)), full((1, w)), full((1, w)), full((1, w)),
        ],
        out_specs=[tok, tok, tok, tok, tok, tok2, tok2, tok2],
        out_shape=[one, one, one, one, one,
                   jax.ShapeDtypeStruct((2, t, w), F32),
                   jax.ShapeDtypeStruct((2, t, w), BF16),
                   jax.ShapeDtypeStruct((2, t, w), BF16)],
        compiler_params=_cparams(("parallel",)),
        name="rwkv_prep",
    )(p, p, p, mu, w2p, a2p, g2p, w0, a0, k_k, k_a, r_k)


RWKV_TILE_PAIRS = 8


def _stack_heads(x):
    lane = lax.broadcasted_iota(jnp.int32, (1, LANE), 1)
    zero = jnp.zeros_like(x)
    return jnp.concatenate([jnp.where(lane < RWKV_HEAD, x, zero), jnp.where(lane >= RWKV_HEAD, x, zero)], axis=0)


def _rwkv_scan_kernel(r_ref, v_ref, kk_ref, lw_ref, kd_ref, b_ref, y_ref, m_ref, lq_s, yl_s, ml_s, el_s):
    d = pl.program_id(0)
    s = pl.program_id(2)
    sign = 1 - 2 * d
    c = RWKV_CHUNK
    tm = TOKEN_TILE
    n_chunks = tm // c
    n_pairs = r_ref.shape[1] // LANE

    @pl.when(s == 0)
    def _():
        m_ref[...] = jnp.zeros_like(m_ref)

    ri = lax.broadcasted_iota(jnp.int32, (tm, tm), 0)
    ci = lax.broadcasted_iota(jnp.int32, (tm, tm), 1)
    same_chunk = (ri // c) == (ci // c)
    along = ((ci - ri) * sign) <= 0
    tri = jnp.where(jnp.logical_and(same_chunk, along), 1.0, 0.0).astype(BF16)
    lw = lw_ref[...]
    lw3 = _split3_bf16(lw)
    lc = sum(_dot(tri, piece) for piece in lw3)
    lend = jnp.concatenate(
        [jnp.broadcast_to(jnp.where(d == 1, lc[cc * c:cc * c + 1], lc[(cc + 1) * c - 1:(cc + 1) * c]),
                          (c, lc.shape[1])) for cc in range(n_chunks)], axis=0)
    e_neg = jnp.exp(-lc)
    e_end = jnp.exp(lend - lc)
    kd = kd_ref[...].astype(F32)
    b = b_ref[...].astype(F32)
    rl_all = r_ref[...].astype(F32) * jnp.exp(lc)
    kke_all = (kk_ref[...].astype(F32) * jnp.exp(lc - lw)).astype(BF16)
    kinv_all = (kd * e_neg).astype(BF16)
    binv_all = (b * e_neg).astype(BF16)
    kdec_all = (kd * e_end).astype(BF16)
    bdec_all = (b * e_end).astype(BF16)

    trow = lax.broadcasted_iota(jnp.int32, (c, LANE), 0)
    tcol = lax.broadcasted_iota(jnp.int32, (c, LANE), 1) & (RWKV_HEAD - 1)
    rel = (tcol - trow) * sign
    incl = rel <= 0
    strict = rel < 0
    eye = jnp.where(rel == 0, 1.0, 0.0)
    hi = lax.broadcasted_iota(jnp.int32, (LANE, LANE), 0) < RWKV_HEAD
    hj = lax.broadcasted_iota(jnp.int32, (LANE, LANE), 1) < RWKV_HEAD
    same_head = hi == hj

    units = [(cc, pr) for cc in range(n_chunks) for pr in range(n_pairs)]

    def sl(x, u):
        cc, pr = u
        return x[cc * c:(cc + 1) * c, pr * LANE:(pr + 1) * LANE]

    rl = [sl(rl_all, u) for u in units]
    kke = [sl(kke_all, u) for u in units]
    vb = [v_ref[u[0] * c:(u[0] + 1) * c, u[1] * LANE:(u[1] + 1) * LANE] for u in units]
    vst = [_stack_heads(x) for x in vb]
    n_units = len(units)
    aa = [_dot_nt(jnp.concatenate([kke[i], rl[i].astype(BF16)], axis=0),
                  jnp.concatenate([_stack_heads(sl(kinv_all, u)), _stack_heads(sl(binv_all, u))], axis=0))
          for i, u in enumerate(units)]
    a_kr = [jnp.concatenate([jnp.where(strict, x[:c, :LANE], 0.0), jnp.where(incl, x[c:, :LANE], 0.0)],
                            axis=0).astype(BF16) for x in aa]
    a_rb = [jnp.where(incl, x[c:, LANE:], 0.0).astype(BF16) for x in aa]
    pw = [jnp.where(strict, -x[:c, LANE:], 0.0) for x in aa]
    tinv = [eye + x for x in pw]
    pwb = [x.astype(BF16) for x in pw]
    pw = [_dot(x, _stack_heads(x)) for x in pwb]
    for _ in range(4):
        pwb = [x.astype(BF16) for x in pw]
        both = [_dot(jnp.concatenate([x, t.astype(BF16)], axis=0), _stack_heads(x)) for x, t in zip(pwb, tinv)]
        pw = [x[:c] for x in both]
        tinv = [t + x[c:] for t, x in zip(tinv, both)]
    tb = [(t + _dot(t.astype(BF16), _stack_heads(x.astype(BF16)))).astype(BF16) for t, x in zip(tinv, pw)]
    avr = [_dot(a, x) for a, x in zip(a_kr, vst)]
    w12 = [_dot(tb[i], jnp.concatenate([_stack_heads(kke[i]), _stack_heads(avr[i][:c].astype(BF16))], axis=1))
           .astype(BF16) for i in range(n_units)]
    corr = [_dot(a_rb[i], jnp.concatenate([_stack_heads(w12[i][:, :LANE]), _stack_heads(w12[i][:, LANE:])], axis=1))
            for i in range(n_units)]
    ones = jnp.ones((3 * c, LANE), BF16)
    for i, u in enumerate(units):
        cc, pr = u
        bw = _dot_tn(sl(bdec_all, u), w12[i])
        lq = jnp.concatenate([rl[i] - corr[i][:, :LANE], jnp.where(same_head, -bw[:, :LANE], 0.0)], axis=0)
        l_hi, l_lo = _split_bf16(lq)
        lq_s[cc, pr] = jnp.concatenate([l_hi, l_lo], axis=1)
        yl_s[cc, pr] = avr[i][c:] - corr[i][:, LANE:]
        ml_s[cc, pr] = jnp.where(same_head, _dot_tn(sl(kdec_all, u), vb[i]) - bw[:, LANE:], 0.0)
        el_s[cc, pr] = jnp.exp(_dot_tn(jnp.concatenate([sl(piece, u) for piece in lw3], axis=0), ones))

    m = [m_ref[pr] for pr in range(n_pairs)]
    for step in range(n_chunks):
        cc = jnp.where(d == 1, n_chunks - 1 - step, step)
        rows = pl.ds(pl.multiple_of(cc * c, c), c)
        for pr in range(n_pairs):
            m_hi, m_lo = _split_bf16(m[pr])
            lq = lq_s[cc, pr]
            res = _dot(lq, jnp.concatenate([m_hi, m_hi], axis=0)) + _dot(lq[:, :LANE], m_lo)
            y_ref[rows, pr * LANE:(pr + 1) * LANE] = (res[:c] + yl_s[cc, pr]).astype(y_ref.dtype)
            m[pr] = el_s[cc, pr] * m[pr] + res[c:] + ml_s[cc, pr]
    for pr in range(n_pairs):
        m_ref[pr] = m[pr]


def _rwkv_scan(r, v, kk, lw, kd, b):
    t, w = r.shape
    nt = t // TOKEN_TILE
    wb = RWKV_TILE_PAIRS * LANE
    n_chunks = TOKEN_TILE // RWKV_CHUNK
    n_pairs = RWKV_TILE_PAIRS

    def tile(d, s):
        return jnp.where(jnp.logical_and(d == 1, s > 0), nt - s, s)

    shared = pl.BlockSpec((TOKEN_TILE, wb), lambda d, j, s: (tile(d, s), j))
    per_dir = pl.BlockSpec((None, TOKEN_TILE, wb), lambda d, j, s: (d, tile(d, s), j))
    unit = lambda rows: pltpu.VMEM((n_chunks, n_pairs, rows, LANE), F32)
    return pl.pallas_call(
        _rwkv_scan_kernel,
        grid=(2, w // wb, nt),
        in_specs=[shared, shared, shared, per_dir, per_dir, per_dir],
        out_specs=per_dir,
        out_shape=jax.ShapeDtypeStruct((2, t, w), BF16),
        scratch_shapes=[pltpu.VMEM((n_pairs, LANE, LANE), F32),
                        pltpu.VMEM((n_chunks, n_pairs, RWKV_CHUNK + LANE, 2 * LANE), BF16),
                        unit(RWKV_CHUNK), unit(LANE), unit(LANE)],
        compiler_params=_cparams(("parallel", "parallel", "arbitrary")),
        name="rwkv_scan",
    )(r, v, kk, lw, kd, b)


def _rotate(t, cos, sin_signed):
    lane = lax.broadcasted_iota(jnp.int32, (1, LANE), 1)
    first = (lane & 63) < 32
    swapped = jnp.where(first, pltpu.roll(t, LANE - 32, axis=1), pltpu.roll(t, 32, axis=1))
    return t * cos + swapped * sin_signed


RET_TILE_HEADS = 2


def _ret_qk(q_ref, k_ref, cos_ref, sin_ref, rows, lanes):
    cos = cos_ref[rows, :]
    sin = sin_ref[rows, :]
    q = _rotate(q_ref[rows, lanes].astype(F32), cos, sin)
    k = _rotate(k_ref[rows, lanes].astype(F32) * (RET_HEAD ** -0.5), cos, sin)
    return q, k


def _log_gamma(l2d_ref, hh):
    lg = jnp.log1p(-jnp.exp2(-l2d_ref[hh:hh + 1, :]))
    return jnp.broadcast_to(lg, (RET_CHUNK, LANE))


def _ret_units(reverse, tile=TOKEN_TILE):
    chunks = range(tile // RET_CHUNK)
    for hh in range(RET_TILE_HEADS):
        for cc in (reversed(chunks) if reverse else chunks):
            yield hh, slice(cc * RET_CHUNK, (cc + 1) * RET_CHUNK), slice(hh * LANE, (hh + 1) * LANE)


def _ret_fwd_kernel(q_ref, k_ref, v_ref, cos_ref, sin_ref, l2d_ref, y_ref, s_ref):
    @pl.when(pl.program_id(1) == 0)
    def _():
        s_ref[...] = jnp.zeros_like(s_ref)

    c = RET_CHUNK
    idx = lax.broadcasted_iota(jnp.int32, (c, LANE), 0).astype(F32)
    for hh, rows, lanes in _ret_units(False, q_ref.shape[0]):
        q, k = _ret_qk(q_ref, k_ref, cos_ref, sin_ref, rows, lanes)
        lg = _log_gamma(l2d_ref, hh)
        s = s_ref[hh]
        y_ref[rows, lanes] = _dot((q * jnp.exp((idx + 1.0) * lg)).astype(BF16), s.astype(BF16)).astype(y_ref.dtype)
        kz = (k * jnp.exp((c - 1.0 - idx) * lg)).astype(BF16)
        s_ref[hh] = s * jnp.exp(c * lg) + _dot_tn(kz, v_ref[rows, lanes])


def _ret_bwd_kernel(q_ref, k_ref, v_ref, g_ref, y1_ref, cos_ref, sin_ref, l2f_ref, l2b_ref, gn_ref, o_ref, s_ref):
    @pl.when(pl.program_id(1) == 0)
    def _():
        s_ref[...] = jnp.zeros_like(s_ref)

    c = RET_CHUNK
    rowi = lax.broadcasted_iota(jnp.int32, (c, LANE), 0)
    coli = lax.broadcasted_iota(jnp.int32, (c, LANE), 1)
    diff = (rowi - coli).astype(F32)
    idx = rowi.astype(F32)
    for hh, rows, lanes in _ret_units(True):
        q, k = _ret_qk(q_ref, k_ref, cos_ref, sin_ref, rows, lanes)
        v = v_ref[rows, lanes]
        lgf = _log_gamma(l2f_ref, hh)
        lgb = _log_gamma(l2b_ref, hh)
        dmat = jnp.where(rowi >= coli, jnp.exp(diff * lgf), jnp.exp(-diff * lgb))
        scores = (_dot_nt(q.astype(BF16), k.astype(BF16)) * dmat).astype(BF16)
        s = s_ref[hh]
        y = (y1_ref[rows, lanes].astype(F32) + _dot(scores, v)
             + _dot((q * jnp.exp((c - idx) * lgb)).astype(BF16), s.astype(BF16)))
        s_ref[hh] = s * jnp.exp(c * lgb) + _dot_tn((k * jnp.exp(idx * lgb)).astype(BF16), v)
        yn = y * lax.rsqrt(jnp.mean(y * y, axis=-1, keepdims=True) + NORM_EPS)
        g = g_ref[rows, lanes].astype(F32)
        o_ref[rows, lanes] = (yn * gn_ref[:, lanes] * (g * jax.nn.sigmoid(g))).astype(o_ref.dtype)


def _retention(p, cos, sin, l2d, gn_w):
    t = p.shape[0]
    nt = t // TOKEN_TILE
    wb = RET_TILE_HEADS * LANE
    n_col = RET_WIDTH // wb
    base = 0
    l2d = l2d.reshape(2, n_col, RET_TILE_HEADS, LANE)

    def rtile(s):
        return jnp.where(s > 0, nt - s, s)

    blk = (TOKEN_TILE, wb)
    rope = (TOKEN_TILE, LANE)
    dec = lambda d: pl.BlockSpec((None, None, RET_TILE_HEADS, LANE), lambda j, s: (d, j, 0, 0))
    rcol = lambda part: pl.BlockSpec(blk, lambda j, s: (rtile(s), base + part * n_col + j))
    scratch = [pltpu.VMEM((RET_TILE_HEADS, RET_HEAD, RET_HEAD), F32)]
    tf = _row_tile(t, 1280)
    fcol = lambda part: pl.BlockSpec((tf, wb), lambda j, s: (s, base + part * n_col + j))
    y1 = pl.pallas_call(
        _ret_fwd_kernel,
        grid=(n_col, t // tf),
        in_specs=[fcol(0), fcol(1), fcol(2), pl.BlockSpec((tf, LANE), lambda j, s: (s, 0)),
                  pl.BlockSpec((tf, LANE), lambda j, s: (s, 0)), dec(0)],
        out_specs=pl.BlockSpec((tf, wb), lambda j, s: (s, j)),
        out_shape=jax.ShapeDtypeStruct((t, RET_WIDTH), BF16),
        scratch_shapes=scratch,
        compiler_params=_cparams(("parallel", "arbitrary")),
        name="ret_fwd",
    )(p, p, p, cos, sin, l2d)
    return pl.pallas_call(
        _ret_bwd_kernel,
        grid=(n_col, nt),
        in_specs=[rcol(0), rcol(1), rcol(2), rcol(3), pl.BlockSpec(blk, lambda j, s: (rtile(s), j)),
                  pl.BlockSpec(rope, lambda j, s: (rtile(s), 0)), pl.BlockSpec(rope, lambda j, s: (rtile(s), 0)),
                  dec(0), dec(1), pl.BlockSpec((1, wb), lambda j, s: (0, j))],
        out_specs=pl.BlockSpec(blk, lambda j, s: (rtile(s), j)),
        out_shape=jax.ShapeDtypeStruct((t, RET_WIDTH), BF16),
        scratch_shapes=scratch,
        compiler_params=_cparams(("parallel", "arbitrary")),
        name="ret_bwd",
    )(p, p, p, p, y1, cos, sin, l2d, l2d, gn_w.reshape(1, RET_WIDTH))


def _mix_out_kernel(ya_ref, bonus_ref, g_ref, yb_ref, x_ref, wo_ref, lnw_ref, lnb_ref, gt_ref, n2_ref, sc_ref,
                    sh_ref, rw_ref, x_out, h_out, s_out):
    is_ctx = pl.program_id(0) == 0
    y = ya_ref[0].astype(F32) + ya_ref[1].astype(F32)
    inv_n = 1.0 / RWKV_HEAD
    mu = _head_sum(y, RWKV_HEAD) * inv_n
    yc = y - mu
    var = _head_sum(yc * yc, RWKV_HEAD) * inv_n
    yn = yc * lax.rsqrt(var + LNX_EPS)
    ya = (yn * lnw_ref[...] + lnb_ref[...] + bonus_ref[...].astype(F32)) * g_ref[...].astype(F32)
    z = _dot(ya.astype(BF16), wo_ref[0:RWKV_WIDTH, :]) + _dot(yb_ref[...], wo_ref[RWKV_WIDTH:, :])
    pick = lambda ref: jnp.where(is_ctx, ref[1:2, :], ref[0:1, :])
    xn = x_ref[...] + pick(gt_ref) * z
    x_out[...] = xn
    h = _rms(xn) * n2_ref[...] * (1.0 + pick(sc_ref)) + pick(sh_ref)
    h_out[...] = h
    h_hi, h_lo = _split_bf16(h)
    r_hi, r_lo = _split_bf16(rw_ref[...])
    s_out[...] = jax.nn.sigmoid(_dot(h_hi, r_hi) + _dot(h_hi, r_lo) + _dot(h_lo, r_hi))


def _mix_out(ya, bonus, g, yb, xa, wo, lnw, lnb, gt, n2, sc, sh, rwt):
    t, d = xa.shape
    w = RWKV_WIDTH
    tm = TOKEN_TILE
    row = lambda n: pl.BlockSpec((1, n), lambda i: (0, 0))
    two = pl.BlockSpec((2, d), lambda i: (0, 0))
    tok = lambda n: pl.BlockSpec((tm, n), lambda i: (i, 0))
    return pl.pallas_call(
        _mix_out_kernel,
        grid=(t // tm,),
        in_specs=[pl.BlockSpec((2, tm, w), lambda i: (0, i, 0)), tok(w), tok(w), tok(w), tok(d),
                  pl.BlockSpec((d, d), lambda i: (0, 0)), row(w), row(w), two, row(d), two, two,
                  pl.BlockSpec((d, LANE), lambda i: (0, 0))],
        out_specs=[tok(d), tok(d), tok(LANE)],
        out_shape=[jax.ShapeDtypeStruct((t, d), F32), jax.ShapeDtypeStruct((t, d), F32),
                   jax.ShapeDtypeStruct((t, LANE), F32)],
        compiler_params=_cparams(("parallel",)),
        name="mix_out",
    )(ya, bonus, g, yb, xa, wo, lnw.reshape(1, w), lnb.reshape(1, w), gt, n2.reshape(1, d), sc, sh, rwt)


def _first_argmax(x, idx, n):
    m = jnp.max(x, axis=0, keepdims=True)
    return m, jnp.min(jnp.where(x == m, idx, n), axis=0, keepdims=True)


def _route_kernel(s_ref, rb_ref, e_out, w_out):
    epg = EXPERTS_PER_GROUP
    tn = s_ref.shape[1]
    idx = lax.broadcasted_iota(jnp.int32, (epg, tn), 0)
    neg = -jnp.inf
    best = None
    for gi in range(N_GROUPS):
        sc = s_ref[gi * epg:(gi + 1) * epg, :]
        bi = sc + rb_ref[gi * epg:(gi + 1) * epg, :]
        m1, i1 = _first_argmax(bi, idx, epg)
        m2 = jnp.max(jnp.where(idx == i1, neg, bi), axis=0, keepdims=True)
        gs = m1 + m2
        if best is None:
            best, g_sel, sc_sel, bi_sel = gs, jnp.zeros((1, tn), jnp.int32), sc, bi
        else:
            better = gs > best
            best = jnp.where(better, gs, best)
            g_sel = jnp.where(better, gi, g_sel)
            sc_sel = jnp.where(better, sc, sc_sel)
            bi_sel = jnp.where(better, bi, bi_sel)
    _, l1 = _first_argmax(bi_sel, idx, epg)
    _, l2 = _first_argmax(jnp.where(idx == l1, neg, bi_sel), idx, epg)
    w1 = jnp.sum(jnp.where(idx == l1, sc_sel, 0.0), axis=0, keepdims=True)
    w2 = jnp.sum(jnp.where(idx == l2, sc_sel, 0.0), axis=0, keepdims=True)
    tot = w1 + w2
    e_out[...] = jnp.concatenate([g_sel * epg + l1, g_sel * epg + l2], axis=0)
    w_out[...] = jnp.concatenate([w1 / tot, w2 / tot], axis=0)


def _route(scores_t, router_b):
    t = scores_t.shape[1]
    tn = _row_tile(t, 1280)
    return pl.pallas_call(
        _route_kernel,
        grid=(t // tn,),
        in_specs=[pl.BlockSpec((N_EXPERTS, tn), lambda i: (0, i)), pl.BlockSpec((N_EXPERTS, 1), lambda i: (0, 0))],
        out_specs=[pl.BlockSpec((TOP_K, tn), lambda i: (0, i)), pl.BlockSpec((TOP_K, tn), lambda i: (0, i))],
        out_shape=[jax.ShapeDtypeStruct((TOP_K, t), jnp.int32), jax.ShapeDtypeStruct((TOP_K, t), F32)],
        compiler_params=_cparams(("parallel",)),
        name="route",
    )(scores_t, router_b.reshape(N_EXPERTS, 1))


def _moe_kernel(be_ref, src_ref, nused_ref, h_hbm, w1_ref, w2_ref, o_hbm, x0, x1, y0, y1, w1b, w2b, gsem, ssem):
    blk = pl.program_id(0)
    n_used = nused_ref[0]
    n_blocks = pl.num_programs(0)
    d = x0.shape[1]
    last_tok = h_hbm.shape[0] - 1
    assert TOP_K == 2

    def gather_copy(b, r, xdst, sem):
        tok = jnp.minimum(jnp.right_shift(src_ref[b * MOE_BLOCK + r], 1), last_tok)
        return pltpu.make_async_copy(h_hbm.at[pl.ds(tok, 1)], xdst.at[pl.ds(r, 1)], sem)

    def wait_rows(xbuf, sem):
        pltpu.make_async_copy(h_hbm.at[pl.ds(0, MOE_BLOCK)], xbuf, sem).wait()

    def wait_results(ybuf, sem):
        pltpu.make_async_copy(ybuf, o_hbm.at[pl.ds(0, MOE_BLOCK), pl.ds(0, d)], sem).wait()

    def scatter_rows(b, ysrc, sem, to_spare):
        for r in range(MOE_BLOCK):
            src = src_ref[b * MOE_BLOCK + r]
            tok = jnp.where(to_spare, last_tok + 1 + r, jnp.right_shift(src, 1))
            lane0 = pl.multiple_of(jnp.where(to_spare, 0, (src & 1) * d), d)
            pltpu.make_async_copy(ysrc.at[pl.ds(r, 1)], o_hbm.at[pl.ds(tok, 1), pl.ds(lane0, d)], sem).start()

    def step(par, xcur, xnext, ycur, yprev):
        first = blk == 0

        @pl.when(first)
        def _():
            yprev[...] = jnp.zeros_like(yprev)
            pltpu.make_async_copy(yprev, o_hbm.at[pl.ds(last_tok + 1, MOE_BLOCK), pl.ds(d, d)], ssem.at[par]).start()

            def body(r, carry):
                gather_copy(blk, r, xcur, gsem.at[par]).start()
                return carry

            lax.fori_loop(0, MOE_BLOCK, body, 0, unroll=8)

        @pl.when(jnp.logical_or(first, be_ref[blk] != be_ref[jnp.maximum(blk - 1, 0)]))
        def _():
            w1b[...] = w1_ref[...].astype(BF16)
            w2b[...] = w2_ref[...].astype(BF16)

        wait_rows(xcur, gsem.at[par])
        nxt = jnp.minimum(blk + 1, n_blocks - 1)
        for r in range(MOE_BLOCK):
            gather_copy(nxt, r, xnext, gsem.at[1 - par]).start()
        scatter_rows(jnp.maximum(blk - 1, 0), yprev, ssem.at[1 - par], first)
        gu = _dot(xcur[...].astype(BF16), w1b[...])
        gate = gu[:, :D_EXPERT]
        act = (gate * jax.nn.sigmoid(gate) * gu[:, D_EXPERT:]).astype(BF16)
        wait_results(ycur, ssem.at[par])
        ycur[...] = _dot(act, w2b[...])

        @pl.when(blk == n_used - 1)
        def _():
            scatter_rows(blk, ycur, ssem.at[par], False)
            wait_results(ycur, ssem.at[par])
            wait_results(yprev, ssem.at[1 - par])
            wait_rows(xnext, gsem.at[1 - par])

    @pl.when(jnp.logical_and(blk < n_used, (blk & 1) == 0))
    def _():
        step(0, x0, x1, y0, y1)

    @pl.when(jnp.logical_and(blk < n_used, (blk & 1) == 1))
    def _():
        step(1, x1, x0, y1, y0)


def _moe_experts(h, block_expert, row_src, n_used, w1, w2, layer):
    t, d = h.shape
    n_blocks = block_expert.shape[0]
    once = pl.Buffered(1)
    grid_spec = pltpu.PrefetchScalarGridSpec(
        num_scalar_prefetch=3,
        grid=(n_blocks,),
        in_specs=[
            pl.BlockSpec(memory_space=pl.ANY),
            pl.BlockSpec((None, None, d, 2 * D_EXPERT), lambda b, be, src, nu: (layer, be[b], 0, 0),
                         pipeline_mode=once),
            pl.BlockSpec((None, None, D_EXPERT, d), lambda b, be, src, nu: (layer, be[b], 0, 0)),
        ],
        out_specs=pl.BlockSpec(memory_space=pl.ANY),
        scratch_shapes=[pltpu.VMEM((MOE_BLOCK, d), F32)] * 4 + [
            pltpu.VMEM((d, 2 * D_EXPERT), BF16), pltpu.VMEM((D_EXPERT, d), BF16),
            pltpu.SemaphoreType.DMA((2,)), pltpu.SemaphoreType.DMA((2,))],
    )
    return pl.pallas_call(
        _moe_kernel,
        grid_spec=grid_spec,
        out_shape=jax.ShapeDtypeStruct((t + MOE_BLOCK, TOP_K * d), F32),
        compiler_params=pltpu.CompilerParams(dimension_semantics=("arbitrary",), vmem_limit_bytes=MOE_VMEM_LIMIT),
        name="moe_experts",
    )(block_expert, row_src, n_used, h, w1, w2)


def _moe_plan(expert):
    t = expert.shape[1]
    m = t * TOP_K
    e_flat = expert.T.reshape(-1)
    onehot = (e_flat[:, None] == jnp.arange(N_EXPERTS, dtype=jnp.int32)[None, :]).astype(jnp.int32)
    csum = jnp.cumsum(onehot, axis=0)
    counts = csum[-1]
    padded = (counts + MOE_BLOCK - 1) // MOE_BLOCK * MOE_BLOCK
    pend = jnp.cumsum(padded)
    pstart = pend - padded
    dest = jnp.sum(onehot * (pstart[None, :] + csum - 1), axis=1)
    m_pad = -(-m // MOE_BLOCK) * MOE_BLOCK + N_EXPERTS * MOE_BLOCK
    n_blocks = m_pad // MOE_BLOCK
    spare = (t + jnp.arange(m_pad, dtype=jnp.int32) % MOE_BLOCK) * TOP_K
    row_src = spare.at[dest].set(jnp.arange(m, dtype=jnp.int32), unique_indices=True, mode='promise_in_bounds')
    block_start = jnp.arange(n_blocks, dtype=jnp.int32) * MOE_BLOCK
    owner = jnp.sum((block_start[:, None] >= pend[None, :]).astype(jnp.int32), axis=1)
    block_expert = jnp.minimum(owner, N_EXPERTS - 1)
    n_used = (pend[-1] // MOE_BLOCK).astype(jnp.int32).reshape(1)
    return block_expert.astype(jnp.int32), row_src, n_used


def _combine_kernel(o_ref, w_ref, x_ref, gt_ref, fg_ref, out_ref, *, final, tile_off):
    is_ctx = (pl.program_id(0) + tile_off) == 0
    d = x_ref.shape[1]
    wts = w_ref[...]
    f = o_ref[:, 0:d] * wts[:, 0:1] + o_ref[:, d:2 * d] * wts[:, 1:2]
    gt = jnp.where(is_ctx, gt_ref[1:2, :], gt_ref[0:1, :])
    xn = x_ref[...] + gt * f
    if final:
        xn = _rms(xn) * fg_ref[...]
    out_ref[...] = xn


def _combine(o2, wts, xa, gt, final_g, final):
    t, d = xa.shape
    tm = TOKEN_TILE
    off = 1 if final else 0
    nt = t // tm - off
    return pl.pallas_call(
        functools.partial(_combine_kernel, final=final, tile_off=off),
        grid=(nt,),
        in_specs=[pl.BlockSpec((tm, 2 * d), lambda i: (i + off, 0)), pl.BlockSpec((tm, TOP_K), lambda i: (i + off, 0)),
                  pl.BlockSpec((tm, d), lambda i: (i + off, 0)), pl.BlockSpec((2, d), lambda i: (0, 0)),
                  pl.BlockSpec((1, d), lambda i: (0, 0))],
        out_specs=pl.BlockSpec((tm, d), lambda i: (i, 0)),
        out_shape=jax.ShapeDtypeStruct((nt * tm, d), F32),
        compiler_params=_cparams(("parallel",)),
        name="combine",
    )(o2, wts, xa, gt, final_g.reshape(1, d))


def _rope_tables(n):
    pos = jnp.arange(n)
    half = RET_HEAD // 2
    inv = ROPE_BASE ** (-jnp.arange(0, half, 2, dtype=F32) / half)

    def ang(p):
        a = p.astype(F32)[:, None] * inv
        return jnp.concatenate([a, a], axis=-1)

    a = jnp.concatenate([ang(pos // GRID_W), ang(pos % GRID_W)], axis=-1)
    sign = jnp.tile(jnp.concatenate([-jnp.ones((half // 2,), F32), jnp.ones((half // 2,), F32)]), 2)
    cos = jnp.concatenate([jnp.ones((CTX_LEN, RET_HEAD), F32), jnp.cos(a)], axis=0)
    sin = jnp.concatenate([jnp.zeros((CTX_LEN, RET_HEAD), F32), jnp.sin(a) * sign], axis=0)
    return cos, sin


def _pad_rows(w, rows, offset):
    out = jnp.zeros((rows,) + w.shape[1:], w.dtype)
    return lax.dynamic_update_slice_in_dim(out, w, offset, axis=0)


def kernel(x, c, ctx, c_ctx, ada_w, ada_b, norm1_g, norm2_g, w_in, shift_mu, rwkv_w0, rwkv_w2, rwkv_a0, rwkv_a2,
           rwkv_g2, rwkv_k_k, rwkv_k_a, rwkv_r_k, rwkv_lnx_w, rwkv_lnx_b, ret_log2_decay, ret_gn_w, w_out,
           router_w, router_b, moe_w1, moe_w2, final_g):
    b, n, d = x.shape
    assert b == 1 and d == D_MODEL and ctx.shape[1] == CTX_LEN and n % TOKEN_TILE == 0
    depth = ada_w.shape[0]
    w = RWKV_WIDTH

    xa = jnp.concatenate([ctx[0], x[0]], axis=0)
    cct = jnp.stack([c[0], c_ctx], axis=1)
    mod = _ada_mod(cct, ada_w, ada_b).reshape(depth, 2, 6, d)
    cos, sin = _rope_tables(n)
    rwt = jnp.pad(router_w, ((0, 0), (0, LANE - N_EXPERTS)))

    out = None
    for l in range(depth):
        last = l == depth - 1
        sh1, sc1, gt1, sh2, sc2, gt2 = (mod[l, :, i, :] for i in range(6))
        h = _norm_mod(xa, norm1_g[l], sc1, sh1)
        p = _matmul(h, w_in, l, RWKV_PAD, BF16)
        p_ret = _matmul(h, w_in[l:l + 1, :, RWKV_IN:], 0, 4 * RET_WIDTH, BF16)

        mu = jnp.pad(shift_mu[l], ((0, 0), (0, RWKV_PAD - RWKV_IN)))
        w2p = jnp.stack([_pad_rows(rwkv_w2[l, 0], LANE, 0), _pad_rows(rwkv_w2[l, 1], LANE, DECAY_LORA)])
        a2p = jnp.stack([_pad_rows(rwkv_a2[l, 0], LANE, 0), _pad_rows(rwkv_a2[l, 1], LANE, ICLR_LORA)])
        g2p = _pad_rows(rwkv_g2[l], 2 * LANE, 0)
        r, v, kk, g, bonus, lw, kd, bb = _rwkv_prep(
            p, mu, w2p.astype(BF16), a2p.astype(BF16), g2p.astype(BF16), rwkv_w0[l], rwkv_a0[l],
            rwkv_k_k[l].reshape(1, w), rwkv_k_a[l].reshape(1, w), rwkv_r_k[l].reshape(1, w))
        ya = _rwkv_scan(r, v, kk, lw, kd, bb)

        l2d = jnp.broadcast_to(ret_log2_decay[l].astype(F32)[:, :, None], (2, RET_HEADS, LANE))
        yb = _retention(p_ret, cos, sin, l2d, ret_gn_w[l])

        xa, h2, scores_t = _mix_out(ya, bonus, g, yb, xa, w_out[l].astype(BF16), rwkv_lnx_w[l], rwkv_lnx_b[l],
                                    gt1, norm2_g[l], sc2, sh2, rwt)
        expert, gate = _route(scores_t[:, :N_EXPERTS].T, router_b)
        block_expert, row_src, n_used = _moe_plan(expert)
        o2 = _moe_experts(h2, block_expert, row_src, n_used, moe_w1, moe_w2, l)
        res = _combine(o2, gate.T, xa, gt2, final_g, last)
        if last:
            out = res
        else:
            xa = res
    return out[None]
```

```python
import functools
import math

import jax
import jax.numpy as jnp
from jax import lax
from jax.experimental import pallas as pl
from jax.experimental.pallas import tpu as pltpu

F32 = jnp.float32
BF16 = jnp.bfloat16
HIGHEST = lax.Precision.HIGHEST

D_MODEL = 2048
CTX_LEN = 256
GRID_W = 64
NORM_EPS = 1e-6

RWKV_WIDTH = 1024
RWKV_HEAD = 64
DECAY_LORA = 64
ICLR_LORA = 64
GATE_LORA = 160
LNX_EPS = 64e-5
RWKV_IN = 3 * RWKV_WIDTH + 2 * DECAY_LORA + 2 * ICLR_LORA + GATE_LORA
LORA_PAD = 512
RWKV_PAD = 3 * RWKV_WIDTH + LORA_PAD

RET_WIDTH = 1024
RET_HEAD = 128
RET_HEADS = 8
RET_CHUNK = 128
ROPE_BASE = 10000.0

N_EXPERTS = 32
N_GROUPS = 4
EXPERTS_PER_GROUP = 8
TOP_K = 2
D_EXPERT = 1024
MOE_BLOCK = 128

LANE = 128
TOKEN_TILE = 256
RWKV_CHUNK = 64
VMEM_LIMIT = 48 * 1024 * 1024
MOE_VMEM_LIMIT = 56 * 1024 * 1024


def _cparams(sem):
    return pltpu.CompilerParams(dimension_semantics=sem, vmem_limit_bytes=VMEM_LIMIT)


def _dot(a, b, precision=None):
    return jnp.dot(a, b, preferred_element_type=F32, precision=precision)


def _dot_nt(a, b, precision=None):
    return lax.dot_general(a, b, (((1,), (1,)), ((), ())), preferred_element_type=F32, precision=precision)


def _dot_tn(a, b, precision=None):
    return lax.dot_general(a, b, (((0,), (0,)), ((), ())), preferred_element_type=F32, precision=precision)


def _split_bf16(x):
    hi = x.astype(BF16)
    lo = (x - hi.astype(F32)).astype(BF16)
    return hi, lo


def _split3_bf16(x):
    hi = x.astype(BF16)
    rest = x - hi.astype(F32)
    mid = rest.astype(BF16)
    return hi, mid, (rest - mid.astype(F32)).astype(BF16)


def _head_sum(x, head):
    ri = lax.broadcasted_iota(jnp.int32, (LANE, LANE), 0) // head
    ci = lax.broadcasted_iota(jnp.int32, (LANE, LANE), 1) // head
    bd = (ri == ci).astype(BF16)
    outs = []
    for j in range(x.shape[1] // LANE):
        hi, lo = _split_bf16(x[:, j * LANE:(j + 1) * LANE])
        outs.append(_dot(hi, bd) + _dot(lo, bd))
    return outs[0] if len(outs) == 1 else jnp.concatenate(outs, axis=1)


def _ada_kernel(cct_ref, w_ref, b_ref, o_ref):
    k = pl.program_id(2)

    @pl.when(k == 0)
    def _():
        o_ref[0] = jnp.broadcast_to(b_ref[0], o_ref.shape[1:])

    s = cct_ref[...]
    s = s * jax.nn.sigmoid(s)
    w = w_ref[0]
    acc0 = jnp.sum(s[:, 0:1] * w, axis=0, keepdims=True)
    acc1 = jnp.sum(s[:, 1:2] * w, axis=0, keepdims=True)
    o_ref[0] += jnp.concatenate([acc0, acc1], axis=0)


def _ada_mod(cct, ada_w, ada_b):
    depth, d, n6 = ada_w.shape
    tk, tn = 512, 2048
    return pl.pallas_call(
        _ada_kernel,
        grid=(depth, n6 // tn, d // tk),
        in_specs=[
            pl.BlockSpec((tk, 2), lambda l, j, k: (k, 0)),
            pl.BlockSpec((1, tk, tn), lambda l, j, k: (l, k, j)),
            pl.BlockSpec((1, 1, tn), lambda l, j, k: (l, 0, j)),
        ],
        out_specs=pl.BlockSpec((1, 2, tn), lambda l, j, k: (l, 0, j)),
        out_shape=jax.ShapeDtypeStruct((depth, 2, n6), F32),
        compiler_params=_cparams(("parallel", "parallel", "arbitrary")),
        name="ada_mod",
    )(cct, ada_w, ada_b.reshape(depth, 1, n6))


def _rms(x):
    return x * lax.rsqrt(jnp.mean(x * x, axis=-1, keepdims=True) + NORM_EPS)


def _norm_mod_kernel(x_ref, g_ref, sc_ref, sh_ref, o_ref):
    is_ctx = pl.program_id(0) == 0
    y = _rms(x_ref[...]) * g_ref[...]
    sc = jnp.where(is_ctx, sc_ref[1:2, :], sc_ref[0:1, :])
    sh = jnp.where(is_ctx, sh_ref[1:2, :], sh_ref[0:1, :])
    o_ref[...] = (y * (1.0 + sc) + sh).astype(o_ref.dtype)


def _norm_mod(xa, g, sc, sh):
    t, d = xa.shape
    row = pl.BlockSpec((1, d), lambda i: (0, 0))
    two = pl.BlockSpec((2, d), lambda i: (0, 0))
    return pl.pallas_call(
        _norm_mod_kernel,
        grid=(t // TOKEN_TILE,),
        in_specs=[pl.BlockSpec((TOKEN_TILE, d), lambda i: (i, 0)), row, two, two],
        out_specs=pl.BlockSpec((TOKEN_TILE, d), lambda i: (i, 0)),
        out_shape=jax.ShapeDtypeStruct((t, d), BF16),
        compiler_params=_cparams(("parallel",)),
        name="norm_mod",
    )(xa, g.reshape(1, d), sc, sh)


def _matmul_kernel(a_ref, b_ref, o_ref):
    o_ref[...] = _dot(a_ref[...], b_ref[...].astype(BF16)).astype(o_ref.dtype)


def _row_tile(t, cap):
    tm = TOKEN_TILE
    for cand in range(TOKEN_TILE, cap + 1, TOKEN_TILE):
        if t % cand == 0:
            tm = cand
    return tm


def _matmul(a, b, layer, n, out_dtype):
    m, k = a.shape
    tm, tn = _row_tile(m, 1280), 512
    assert n % tn == 0 and n <= b.shape[2]
    return pl.pallas_call(
        _matmul_kernel,
        grid=(m // tm, n // tn),
        in_specs=[pl.BlockSpec((tm, k), lambda i, j: (i, 0)),
                  pl.BlockSpec((None, k, tn), lambda i, j: (layer, 0, j))],
        out_specs=pl.BlockSpec((tm, tn), lambda i, j: (i, j)),
        out_shape=jax.ShapeDtypeStruct((m, n), out_dtype),
        compiler_params=_cparams(("parallel", "arbitrary")),
        name="in_proj",
    )(a, b)


def _rwkv_prep_kernel(pm_ref, pp_ref, pn_ref, mu_ref, w2_ref, a2_ref, g2_ref, w0_ref, a0_ref, kk_ref, ka_ref,
                      rk_ref, r_out, v_out, kk_out, g_out, bonus_out, lw_out, kd_out, b_out):
    i = pl.program_id(0)
    nt = pl.num_programs(0)
    is_ctx = i == 0
    tm = TOKEN_TILE
    row = lax.broadcasted_iota(jnp.int32, (tm, 1), 0)
    in_row = row & (GRID_W - 1)
    edge_lo = jnp.where(is_ctx, row, in_row) == 0
    edge_hi = jnp.where(is_ctx, row - (tm - 1), in_row - (GRID_W - 1)) == 0
    no_up = jnp.logical_or(is_ctx, i == 1)
    no_down = jnp.logical_or(is_ctx, i == nt - 1)
    x_gate = jnp.where(is_ctx, 0.0, 1.0)

    ri = lax.broadcasted_iota(jnp.int32, (tm, tm), 0)
    ci = lax.broadcasted_iota(jnp.int32, (tm, tm), 1)
    sel_prev = jnp.where(ci == ri - 1, jnp.where(edge_lo, 0.0, 1.0), 0.0).astype(BF16)
    sel_next = jnp.where(ci == ri + 1, jnp.where(edge_hi, 0.0, 1.0), 0.0).astype(BF16)

    def shifted(c0, c1):
        pb = pm_ref[:, c0:c1]
        p = pb.astype(F32)
        mu = mu_ref[:, c0:c1]
        prev = _dot(sel_prev, pb)
        nxt = _dot(sel_next, pb)
        out = p + mu[0:1] * (prev - p) + mu[1:2] * (nxt - p)
        above = jnp.where(no_up, 0.0, pp_ref[:, c0:c1].astype(F32))
        below = jnp.where(no_down, 0.0, pn_ref[:, c0:c1].astype(F32))
        up = jnp.concatenate([above, p[:tm - GRID_W]], axis=0)
        down = jnp.concatenate([p[GRID_W:], below], axis=0)
        return out + x_gate * (mu[2:3] * (up - p) + mu[3:4] * (down - p))

    w = RWKV_WIDTH
    ul = shifted(3 * w, 3 * w + LORA_PAD)
    twd = jnp.tanh(ul[:, 0:LANE]).astype(BF16)
    ad = ul[:, LANE:2 * LANE].astype(BF16)
    sg = jax.nn.sigmoid(ul[:, 2 * LANE:4 * LANE]).astype(BF16)
    g_out[...] = _dot(sg, g2_ref[...]).astype(g_out.dtype)

    r = shifted(0, w)
    k = shifted(w, 2 * w)
    v = shifted(2 * w, 3 * w)
    r_out[...] = r.astype(r_out.dtype)
    v_out[...] = v.astype(v_out.dtype)
    kkr = k * kk_ref[...]
    nrm = jnp.sqrt(_head_sum(kkr * kkr, RWKV_HEAD))
    kk = kkr / jnp.maximum(nrm, 1e-12)
    kk_out[...] = kk.astype(kk_out.dtype)
    ksum = jnp.zeros_like(k)
    for d in range(2):
        z = w0_ref[d:d + 1, :] + _dot(twd, w2_ref[d])
        lw_out[d] = -jax.nn.sigmoid(z) * math.exp(-0.5)
        a = jax.nn.sigmoid(a0_ref[d:d + 1, :] + _dot(ad, a2_ref[d]))
        kd = k * (1.0 + (a - 1.0) * ka_ref[...])
        kd_out[d] = kd.astype(kd_out.dtype)
        b_out[d] = (a * kk).astype(b_out.dtype)
        ksum = ksum + kd
    bonus = _head_sum(r * ksum * rk_ref[...], RWKV_HEAD) * v
    bonus_out[...] = bonus.astype(bonus_out.dtype)


def _rwkv_prep(p, mu, w2p, a2p, g2p, w0, a0, k_k, k_a, r_k):
    t = p.shape[0]
    nt = t // TOKEN_TILE
    per_tile = TOKEN_TILE // GRID_W
    n_rows = t // GRID_W
    w = RWKV_WIDTH
    full = lambda shape: pl.BlockSpec(shape, lambda i: (0,) * len(shape))
    tok = pl.BlockSpec((TOKEN_TILE, w), lambda i: (i, 0))
    tok2 = pl.BlockSpec((2, TOKEN_TILE, w), lambda i: (0, i, 0))
    one = jax.ShapeDtypeStruct((t, w), BF16)
    return pl.pallas_call(
        _rwkv_prep_kernel,
        grid=(nt,),
        in_specs=[
            pl.BlockSpec((TOKEN_TILE, RWKV_PAD), lambda i: (i, 0)),
            pl.BlockSpec((GRID_W, RWKV_PAD), lambda i: (jnp.maximum(i * per_tile - 1, 0), 0)),
            pl.BlockSpec((GRID_W, RWKV_PAD), lambda i: (jnp.minimum((i + 1) * per_tile, n_rows - 1), 0)),
            full((4, RWKV_PAD)), full((2, LANE, w)), full((2, LANE, w)), full((2 * LANE, w)),
            full((2, w)), full((2, w)), full((1, w)), full((1, w)), full((1, w)),
        ],
        out_specs=[tok, tok, tok, tok, tok, tok2, tok2, tok2],
        out_shape=[one, one, one, one, one,
                   jax.ShapeDtypeStruct((2, t, w), F32),
                   jax.ShapeDtypeStruct((2, t, w), BF16),
                   jax.ShapeDtypeStruct((2, t, w), BF16)],
        compiler_params=_cparams(("parallel",)),
        name="rwkv_prep",
    )(p, p, p, mu, w2p, a2p, g2p, w0, a0, k_k, k_a, r_k)


RWKV_TILE_PAIRS = 8


def _stack_heads(x):
    lane = lax.broadcasted_iota(jnp.int32, (1, LANE), 1)
    zero = jnp.zeros_like(x)
    return jnp.concatenate([jnp.where(lane < RWKV_HEAD, x, zero), jnp.where(lane >= RWKV_HEAD, x, zero)], axis=0)


def _rwkv_scan_kernel(r_ref, v_ref, kk_ref, lw_ref, kd_ref, b_ref, y_ref, m_ref, lq_s, yl_s, ml_s, el_s):
    d = pl.program_id(0)
    s = pl.program_id(2)
    sign = 1 - 2 * d
    c = RWKV_CHUNK
    tm = TOKEN_TILE
    n_chunks = tm // c
    n_pairs = r_ref.shape[1] // LANE

    @pl.when(s == 0)
    def _():
        m_ref[...] = jnp.zeros_like(m_ref)

    ri = lax.broadcasted_iota(jnp.int32, (tm, tm), 0)
    ci = lax.broadcasted_iota(jnp.int32, (tm, tm), 1)
    same_chunk = (ri // c) == (ci // c)
    along = ((ci - ri) * sign) <= 0
    tri = jnp.where(jnp.logical_and(same_chunk, along), 1.0, 0.0).astype(BF16)
    lw = lw_ref[...]
    lw3 = _split3_bf16(lw)
    lc = sum(_dot(tri, piece) for piece in lw3)
    lend = jnp.concatenate(
        [jnp.broadcast_to(jnp.where(d == 1, lc[cc * c:cc * c + 1], lc[(cc + 1) * c - 1:(cc + 1) * c]),
                          (c, lc.shape[1])) for cc in range(n_chunks)], axis=0)
    e_neg = jnp.exp(-lc)
    e_end = jnp.exp(lend - lc)
    kd = kd_ref[...].astype(F32)
    b = b_ref[...].astype(F32)
    rl_all = r_ref[...].astype(F32) * jnp.exp(lc)
    kke_all = (kk_ref[...].astype(F32) * jnp.exp(lc - lw)).astype(BF16)
    kinv_all = (kd * e_neg).astype(BF16)
    binv_all = (b * e_neg).astype(BF16)
    kdec_all = (kd * e_end).astype(BF16)
    bdec_all = (b * e_end).astype(BF16)

    trow = lax.broadcasted_iota(jnp.int32, (c, LANE), 0)
    tcol = lax.broadcasted_iota(jnp.int32, (c, LANE), 1) & (RWKV_HEAD - 1)
    rel = (tcol - trow) * sign
    incl = rel <= 0
    strict = rel < 0
    eye = jnp.where(rel == 0, 1.0, 0.0)
    hi = lax.broadcasted_iota(jnp.int32, (LANE, LANE), 0) < RWKV_HEAD
    hj = lax.broadcasted_iota(jnp.int32, (LANE, LANE), 1) < RWKV_HEAD
    same_head = hi == hj

    units = [(cc, pr) for cc in range(n_chunks) for pr in range(n_pairs)]

    def sl(x, u):
        cc, pr = u
        return x[cc * c:(cc + 1) * c, pr * LANE:(pr + 1) * LANE]

    rl = [sl(rl_all, u) for u in units]
    kke = [sl(kke_all, u) for u in units]
    vb = [v_ref[u[0] * c:(u[0] + 1) * c, u[1] * LANE:(u[1] + 1) * LANE] for u in units]
    vst = [_stack_heads(x) for x in vb]
    n_units = len(units)
    aa = [_dot_nt(jnp.concatenate([kke[i], rl[i].astype(BF16)], axis=0),
                  jnp.concatenate([_stack_heads(sl(kinv_all, u)), _stack_heads(sl(binv_all, u))], axis=0))
          for i, u in enumerate(units)]
    a_kr = [jnp.concatenate([jnp.where(strict, x[:c, :LANE], 0.0), jnp.where(incl, x[c:, :LANE], 0.0)],
                            axis=0).astype(BF16) for x in aa]
    a_rb = [jnp.where(incl, x[c:, LANE:], 0.0).astype(BF16) for x in aa]
    pw = [jnp.where(strict, -x[:c, LANE:], 0.0) for x in aa]
    tinv = [eye + x for x in pw]
    pwb = [x.astype(BF16) for x in pw]
    pw = [_dot(x, _stack_heads(x)) for x in pwb]
    for _ in range(4):
        pwb = [x.astype(BF16) for x in pw]
        both = [_dot(jnp.concatenate([x, t.astype(BF16)], axis=0), _stack_heads(x)) for x, t in zip(pwb, tinv)]
        pw = [x[:c] for x in both]
        tinv = [t + x[c:] for t, x in zip(tinv, both)]
    tb = [(t + _dot(t.astype(BF16), _stack_heads(x.astype(BF16)))).astype(BF16) for t, x in zip(tinv, pw)]
    avr = [_dot(a, x) for a, x in zip(a_kr, vst)]
    w12 = [_dot(tb[i], jnp.concatenate([_stack_heads(kke[i]), _stack_heads(avr[i][:c].astype(BF16))], axis=1))
           .astype(BF16) for i in range(n_units)]
    corr = [_dot(a_rb[i], jnp.concatenate([_stack_heads(w12[i][:, :LANE]), _stack_heads(w12[i][:, LANE:])], axis=1))
            for i in range(n_units)]
    ones = jnp.ones((3 * c, LANE), BF16)
    for i, u in enumerate(units):
        cc, pr = u
        bw = _dot_tn(sl(bdec_all, u), w12[i])
        lq = jnp.concatenate([rl[i] - corr[i][:, :LANE], jnp.where(same_head, -bw[:, :LANE], 0.0)], axis=0)
        l_hi, l_lo = _split_bf16(lq)
        lq_s[cc, pr] = jnp.concatenate([l_hi, l_lo], axis=1)
        yl_s[cc, pr] = avr[i][c:] - corr[i][:, LANE:]
        ml_s[cc, pr] = jnp.where(same_head, _dot_tn(sl(kdec_all, u), vb[i]) - bw[:, LANE:], 0.0)
        el_s[cc, pr] = jnp.exp(_dot_tn(jnp.concatenate([sl(piece, u) for piece in lw3], axis=0), ones))

    m = [m_ref[pr] for pr in range(n_pairs)]
    for step in range(n_chunks):
        cc = jnp.where(d == 1, n_chunks - 1 - step, step)
        rows = pl.ds(pl.multiple_of(cc * c, c), c)
        for pr in range(n_pairs):
            m_hi, m_lo = _split_bf16(m[pr])
            lq = lq_s[cc, pr]
            res = _dot(lq, jnp.concatenate([m_hi, m_hi], axis=0)) + _dot(lq[:, :LANE], m_lo)
            y_ref[rows, pr * LANE:(pr + 1) * LANE] = (res[:c] + yl_s[cc, pr]).astype(y_ref.dtype)
            m[pr] = el_s[cc, pr] * m[pr] + res[c:] + ml_s[cc, pr]
    for pr in range(n_pairs):
        m_ref[pr] = m[pr]


def _rwkv_scan(r, v, kk, lw, kd, b):
    t, w = r.shape
    nt = t // TOKEN_TILE
    wb = RWKV_TILE_PAIRS * LANE
    n_chunks = TOKEN_TILE // RWKV_CHUNK
    n_pairs = RWKV_TILE_PAIRS

    def tile(d, s):
        return jnp.where(jnp.logical_and(d == 1, s > 0), nt - s, s)

    shared = pl.BlockSpec((TOKEN_TILE, wb), lambda d, j, s: (tile(d, s), j))
    per_dir = pl.BlockSpec((None, TOKEN_TILE, wb), lambda d, j, s: (d, tile(d, s), j))
    unit = lambda rows: pltpu.VMEM((n_chunks, n_pairs, rows, LANE), F32)
    return pl.pallas_call(
        _rwkv_scan_kernel,
        grid=(2, w // wb, nt),
        in_specs=[shared, shared, shared, per_dir, per_dir, per_dir],
        out_specs=per_dir,
        out_shape=jax.ShapeDtypeStruct((2, t, w), BF16),
        scratch_shapes=[pltpu.VMEM((n_pairs, LANE, LANE), F32),
                        pltpu.VMEM((n_chunks, n_pairs, RWKV_CHUNK + LANE, 2 * LANE), BF16),
                        unit(RWKV_CHUNK), unit(LANE), unit(LANE)],
        compiler_params=_cparams(("parallel", "parallel", "arbitrary")),
        name="rwkv_scan",
    )(r, v, kk, lw, kd, b)


def _rotate(t, cos, sin_signed):
    lane = lax.broadcasted_iota(jnp.int32, (1, LANE), 1)
    first = (lane & 63) < 32
    swapped = jnp.where(first, pltpu.roll(t, LANE - 32, axis=1), pltpu.roll(t, 32, axis=1))
    return t * cos + swapped * sin_signed


RET_TILE_HEADS = 2


def _ret_qk(q_ref, k_ref, cos_ref, sin_ref, rows, lanes):
    cos = cos_ref[rows, :]
    sin = sin_ref[rows, :]
    q = _rotate(q_ref[rows, lanes].astype(F32), cos, sin)
    k = _rotate(k_ref[rows, lanes].astype(F32) * (RET_HEAD ** -0.5), cos, sin)
    return q, k


def _log_gamma(l2d_ref, hh):
    lg = jnp.log1p(-jnp.exp2(-l2d_ref[hh:hh + 1, :]))
    return jnp.broadcast_to(lg, (RET_CHUNK, LANE))


def _ret_units(reverse, tile=TOKEN_TILE):
    chunks = range(tile // RET_CHUNK)
    for hh in range(RET_TILE_HEADS):
        for cc in (reversed(chunks) if reverse else chunks):
            yield hh, slice(cc * RET_CHUNK, (cc + 1) * RET_CHUNK), slice(hh * LANE, (hh + 1) * LANE)


def _ret_fwd_kernel(q_ref, k_ref, v_ref, cos_ref, sin_ref, l2d_ref, y_ref, s_ref):
    @pl.when(pl.program_id(1) == 0)
    def _():
        s_ref[...] = jnp.zeros_like(s_ref)

    c = RET_CHUNK
    idx = lax.broadcasted_iota(jnp.int32, (c, LANE), 0).astype(F32)
    for hh, rows, lanes in _ret_units(False, q_ref.shape[0]):
        q, k = _ret_qk(q_ref, k_ref, cos_ref, sin_ref, rows, lanes)
        lg = _log_gamma(l2d_ref, hh)
        s = s_ref[hh]
        y_ref[rows, lanes] = _dot((q * jnp.exp((idx + 1.0) * lg)).astype(BF16), s.astype(BF16)).astype(y_ref.dtype)
        kz = (k * jnp.exp((c - 1.0 - idx) * lg)).astype(BF16)
        s_ref[hh] = s * jnp.exp(c * lg) + _dot_tn(kz, v_ref[rows, lanes])


def _ret_bwd_kernel(q_ref, k_ref, v_ref, g_ref, y1_ref, cos_ref, sin_ref, l2f_ref, l2b_ref, gn_ref, o_ref, s_ref):
    @pl.when(pl.program_id(1) == 0)
    def _():
        s_ref[...] = jnp.zeros_like(s_ref)

    c = RET_CHUNK
    rowi = lax.broadcasted_iota(jnp.int32, (c, LANE), 0)
    coli = lax.broadcasted_iota(jnp.int32, (c, LANE), 1)
    diff = (rowi - coli).astype(F32)
    idx = rowi.astype(F32)
    for hh, rows, lanes in _ret_units(True):
        q, k = _ret_qk(q_ref, k_ref, cos_ref, sin_ref, rows, lanes)
        v = v_ref[rows, lanes]
        lgf = _log_gamma(l2f_ref, hh)
        lgb = _log_gamma(l2b_ref, hh)
        dmat = jnp.where(rowi >= coli, jnp.exp(diff * lgf), jnp.exp(-diff * lgb))
        scores = (_dot_nt(q.astype(BF16), k.astype(BF16)) * dmat).astype(BF16)
        s = s_ref[hh]
        y = (y1_ref[rows, lanes].astype(F32) + _dot(scores, v)
             + _dot((q * jnp.exp((c - idx) * lgb)).astype(BF16), s.astype(BF16)))
        s_ref[hh] = s * jnp.exp(c * lgb) + _dot_tn((k * jnp.exp(idx * lgb)).astype(BF16), v)
        yn = y * lax.rsqrt(jnp.mean(y * y, axis=-1, keepdims=True) + NORM_EPS)
        g = g_ref[rows, lanes].astype(F32)
        o_ref[rows, lanes] = (yn * gn_ref[:, lanes] * (g * jax.nn.sigmoid(g))).astype(o_ref.dtype)


def _retention(p, cos, sin, l2d, gn_w):
    t = p.shape[0]
    nt = t // TOKEN_TILE
    wb = RET_TILE_HEADS * LANE
    n_col = RET_WIDTH // wb
    base = 0
    l2d = l2d.reshape(2, n_col, RET_TILE_HEADS, LANE)

    def rtile(s):
        return jnp.where(s > 0, nt - s, s)

    blk = (TOKEN_TILE, wb)
    rope = (TOKEN_TILE, LANE)
    dec = lambda d: pl.BlockSpec((None, None, RET_TILE_HEADS, LANE), lambda j, s: (d, j, 0, 0))
    rcol = lambda part: pl.BlockSpec(blk, lambda j, s: (rtile(s), base + part * n_col + j))
    scratch = [pltpu.VMEM((RET_TILE_HEADS, RET_HEAD, RET_HEAD), F32)]
    tf = _row_tile(t, 1280)
    fcol = lambda part: pl.BlockSpec((tf, wb), lambda j, s: (s, base + part * n_col + j))
    y1 = pl.pallas_call(
        _ret_fwd_kernel,
        grid=(n_col, t // tf),
        in_specs=[fcol(0), fcol(1), fcol(2), pl.BlockSpec((tf, LANE), lambda j, s: (s, 0)),
                  pl.BlockSpec((tf, LANE), lambda j, s: (s, 0)), dec(0)],
        out_specs=pl.BlockSpec((tf, wb), lambda j, s: (s, j)),
        out_shape=jax.ShapeDtypeStruct((t, RET_WIDTH), BF16),
        scratch_shapes=scratch,
        compiler_params=_cparams(("parallel", "arbitrary")),
        name="ret_fwd",
    )(p, p, p, cos, sin, l2d)
    return pl.pallas_call(
        _ret_bwd_kernel,
        grid=(n_col, nt),
        in_specs=[rcol(0), rcol(1), rcol(2), rcol(3), pl.BlockSpec(blk, lambda j, s: (rtile(s), j)),
                  pl.BlockSpec(rope, lambda j, s: (rtile(s), 0)), pl.BlockSpec(rope, lambda j, s: (rtile(s), 0)),
                  dec(0), dec(1), pl.BlockSpec((1, wb), lambda j, s: (0, j))],
        out_specs=pl.BlockSpec(blk, lambda j, s: (rtile(s), j)),
        out_shape=jax.ShapeDtypeStruct((t, RET_WIDTH), BF16),
        scratch_shapes=scratch,
        compiler_params=_cparams(("parallel", "arbitrary")),
        name="ret_bwd",
    )(p, p, p, p, y1, cos, sin, l2d, l2d, gn_w.reshape(1, RET_WIDTH))


def _mix_out_kernel(ya_ref, bonus_ref, g_ref, yb_ref, x_ref, wo_ref, lnw_ref, lnb_ref, gt_ref, n2_ref, sc_ref,
                    sh_ref, rw_ref, x_out, h_out, s_out):
    is_ctx = pl.program_id(0) == 0
    y = ya_ref[0].astype(F32) + ya_ref[1].astype(F32)
    inv_n = 1.0 / RWKV_HEAD
    mu = _head_sum(y, RWKV_HEAD) * inv_n
    yc = y - mu
    var = _head_sum(yc * yc, RWKV_HEAD) * inv_n
    yn = yc * lax.rsqrt(var + LNX_EPS)
    ya = (yn * lnw_ref[...] + lnb_ref[...] + bonus_ref[...].astype(F32)) * g_ref[...].astype(F32)
    z = _dot(ya.astype(BF16), wo_ref[0:RWKV_WIDTH, :]) + _dot(yb_ref[...], wo_ref[RWKV_WIDTH:, :])
    pick = lambda ref: jnp.where(is_ctx, ref[1:2, :], ref[0:1, :])
    xn = x_ref[...] + pick(gt_ref) * z
    x_out[...] = xn
    h = _rms(xn) * n2_ref[...] * (1.0 + pick(sc_ref)) + pick(sh_ref)
    h_out[...] = h
    h_hi, h_lo = _split_bf16(h)
    r_hi, r_lo = _split_bf16(rw_ref[...])
    s_out[...] = jax.nn.sigmoid(_dot(h_hi, r_hi) + _dot(h_hi, r_lo) + _dot(h_lo, r_hi))


def _mix_out(ya, bonus, g, yb, xa, wo, lnw, lnb, gt, n2, sc, sh, rwt):
    t, d = xa.shape
    w = RWKV_WIDTH
    tm = TOKEN_TILE
    row = lambda n: pl.BlockSpec((1, n), lambda i: (0, 0))
    two = pl.BlockSpec((2, d), lambda i: (0, 0))
    tok = lambda n: pl.BlockSpec((tm, n), lambda i: (i, 0))
    return pl.pallas_call(
        _mix_out_kernel,
        grid=(t // tm,),
        in_specs=[pl.BlockSpec((2, tm, w), lambda i: (0, i, 0)), tok(w), tok(w), tok(w), tok(d),
                  pl.BlockSpec((d, d), lambda i: (0, 0)), row(w), row(w), two, row(d), two, two,
                  pl.BlockSpec((d, LANE), lambda i: (0, 0))],
        out_specs=[tok(d), tok(d), tok(LANE)],
        out_shape=[jax.ShapeDtypeStruct((t, d), F32), jax.ShapeDtypeStruct((t, d), F32),
                   jax.ShapeDtypeStruct((t, LANE), F32)],
        compiler_params=_cparams(("parallel",)),
        name="mix_out",
    )(ya, bonus, g, yb, xa, wo, lnw.reshape(1, w), lnb.reshape(1, w), gt, n2.reshape(1, d), sc, sh, rwt)


def _first_argmax(x, idx, n):
    m = jnp.max(x, axis=0, keepdims=True)
    return m, jnp.min(jnp.where(x == m, idx, n), axis=0, keepdims=True)


def _route_kernel(s_ref, rb_ref, e_out, w_out):
    epg = EXPERTS_PER_GROUP
    tn = s_ref.shape[1]
    idx = lax.broadcasted_iota(jnp.int32, (epg, tn), 0)
    neg = -jnp.inf
    best = None
    for gi in range(N_GROUPS):
        sc = s_ref[gi * epg:(gi + 1) * epg, :]
        bi = sc + rb_ref[gi * epg:(gi + 1) * epg, :]
        m1, i1 = _first_argmax(bi, idx, epg)
        m2 = jnp.max(jnp.where(idx == i1, neg, bi), axis=0, keepdims=True)
        gs = m1 + m2
        if best is None:
            best, g_sel, sc_sel, bi_sel = gs, jnp.zeros((1, tn), jnp.int32), sc, bi
        else:
            better = gs > best
            best = jnp.where(better, gs, best)
            g_sel = jnp.where(better, gi, g_sel)
            sc_sel = jnp.where(better, sc, sc_sel)
            bi_sel = jnp.where(better, bi, bi_sel)
    _, l1 = _first_argmax(bi_sel, idx, epg)
    _, l2 = _first_argmax(jnp.where(idx == l1, neg, bi_sel), idx, epg)
    w1 = jnp.sum(jnp.where(idx == l1, sc_sel, 0.0), axis=0, keepdims=True)
    w2 = jnp.sum(jnp.where(idx == l2, sc_sel, 0.0), axis=0, keepdims=True)
    tot = w1 + w2
    e_out[...] = jnp.concatenate([g_sel * epg + l1, g_sel * epg + l2], axis=0)
    w_out[...] = jnp.concatenate([w1 / tot, w2 / tot], axis=0)


def _route(scores_t, router_b):
    t = scores_t.shape[1]
    tn = _row_tile(t, 1280)
    return pl.pallas_call(
        _route_kernel,
        grid=(t // tn,),
        in_specs=[pl.BlockSpec((N_EXPERTS, tn), lambda i: (0, i)), pl.BlockSpec((N_EXPERTS, 1), lambda i: (0, 0))],
        out_specs=[pl.BlockSpec((TOP_K, tn), lambda i: (0, i)), pl.BlockSpec((TOP_K, tn), lambda i: (0, i))],
        out_shape=[jax.ShapeDtypeStruct((TOP_K, t), jnp.int32), jax.ShapeDtypeStruct((TOP_K, t), F32)],
        compiler_params=_cparams(("parallel",)),
        name="route",
    )(scores_t, router_b.reshape(N_EXPERTS, 1))


def _moe_kernel(be_ref, src_ref, nused_ref, h_hbm, w1_ref, w2_ref, o_hbm, x0, x1, y0, y1, w1b, w2b, gsem, ssem):
    blk = pl.program_id(0)
    n_used = nused_ref[0]
    n_blocks = pl.num_programs(0)
    d = x0.shape[1]
    last_tok = h_hbm.shape[0] - 1
    assert TOP_K == 2

    def gather_copy(b, r, xdst, sem):
        tok = jnp.minimum(jnp.right_shift(src_ref[b * MOE_BLOCK + r], 1), last_tok)
        return pltpu.make_async_copy(h_hbm.at[pl.ds(tok, 1)], xdst.at[pl.ds(r, 1)], sem)

    def wait_rows(xbuf, sem):
        pltpu.make_async_copy(h_hbm.at[pl.ds(0, MOE_BLOCK)], xbuf, sem).wait()

    def wait_results(ybuf, sem):
        pltpu.make_async_copy(ybuf, o_hbm.at[pl.ds(0, MOE_BLOCK), pl.ds(0, d)], sem).wait()

    def scatter_rows(b, ysrc, sem, to_spare):
        for r in range(MOE_BLOCK):
            src = src_ref[b * MOE_BLOCK + r]
            tok = jnp.where(to_spare, last_tok + 1 + r, jnp.right_shift(src, 1))
            lane0 = pl.multiple_of(jnp.where(to_spare, 0, (src & 1) * d), d)
            pltpu.make_async_copy(ysrc.at[pl.ds(r, 1)], o_hbm.at[pl.ds(tok, 1), pl.ds(lane0, d)], sem).start(
                priority=r % 2)

    def step(par, xcur, xnext, ycur, yprev):
        first = blk == 0

        @pl.when(first)
        def _():
            yprev[...] = jnp.zeros_like(yprev)
            pltpu.make_async_copy(yprev, o_hbm.at[pl.ds(last_tok + 1, MOE_BLOCK), pl.ds(d, d)], ssem.at[par]).start()

            def body(r, carry):
                gather_copy(blk, r, xcur, gsem.at[par]).start()
                return carry

            lax.fori_loop(0, MOE_BLOCK, body, 0, unroll=8)

        @pl.when(jnp.logical_or(first, be_ref[blk] != be_ref[jnp.maximum(blk - 1, 0)]))
        def _():
            w1b[...] = w1_ref[...].astype(BF16)
            w2b[...] = w2_ref[...].astype(BF16)

        wait_rows(xcur, gsem.at[par])
        nxt = jnp.minimum(blk + 1, n_blocks - 1)
        for r in range(MOE_BLOCK):
            gather_copy(nxt, r, xnext, gsem.at[1 - par]).start(priority=r % 2)
        scatter_rows(jnp.maximum(blk - 1, 0), yprev, ssem.at[1 - par], first)
        gu = _dot(xcur[...].astype(BF16), w1b[...])
        gate = gu[:, :D_EXPERT]
        act = (gate * jax.nn.sigmoid(gate) * gu[:, D_EXPERT:]).astype(BF16)
        wait_results(ycur, ssem.at[par])
        ycur[...] = _dot(act, w2b[...])

        @pl.when(blk == n_used - 1)
        def _():
            scatter_rows(blk, ycur, ssem.at[par], False)
            wait_results(ycur, ssem.at[par])
            wait_results(yprev, ssem.at[1 - par])
            wait_rows(xnext, gsem.at[1 - par])

    @pl.when(jnp.logical_and(blk < n_used, (blk & 1) == 0))
    def _():
        step(0, x0, x1, y0, y1)

    @pl.when(jnp.logical_and(blk < n_used, (blk & 1) == 1))
    def _():
        step(1, x1, x0, y1, y0)


def _moe_experts(h, block_expert, row_src, n_used, w1, w2, layer):
    t, d = h.shape
    n_blocks = block_expert.shape[0]
    once = pl.Buffered(1)
    grid_spec = pltpu.PrefetchScalarGridSpec(
        num_scalar_prefetch=3,
        grid=(n_blocks,),
        in_specs=[
            pl.BlockSpec(memory_space=pl.ANY),
            pl.BlockSpec((None, None, d, 2 * D_EXPERT), lambda b, be, src, nu: (layer, be[b], 0, 0),
                         pipeline_mode=once),
            pl.BlockSpec((None, None, D_EXPERT, d), lambda b, be, src, nu: (layer, be[b], 0, 0)),
        ],
        out_specs=pl.BlockSpec(memory_space=pl.ANY),
        scratch_shapes=[pltpu.VMEM((MOE_BLOCK, d), F32)] * 4 + [
            pltpu.VMEM((d, 2 * D_EXPERT), BF16), pltpu.VMEM((D_EXPERT, d), BF16),
            pltpu.SemaphoreType.DMA((2,)), pltpu.SemaphoreType.DMA((2,))],
    )
    return pl.pallas_call(
        _moe_kernel,
        grid_spec=grid_spec,
        out_shape=jax.ShapeDtypeStruct((t + MOE_BLOCK, TOP_K * d), F32),
        compiler_params=pltpu.CompilerParams(dimension_semantics=("arbitrary",), vmem_limit_bytes=MOE_VMEM_LIMIT),
        name="moe_experts",
    )(block_expert, row_src, n_used, h, w1, w2)


def _moe_plan(expert):
    t = expert.shape[1]
    m = t * TOP_K
    e_flat = expert.T.reshape(-1)
    onehot = (e_flat[:, None] == jnp.arange(N_EXPERTS, dtype=jnp.int32)[None, :]).astype(jnp.int32)
    csum = jnp.cumsum(onehot, axis=0)
    counts = csum[-1]
    padded = (counts + MOE_BLOCK - 1) // MOE_BLOCK * MOE_BLOCK
    pend = jnp.cumsum(padded)
    pstart = pend - padded
    dest = jnp.sum(onehot * (pstart[None, :] + csum - 1), axis=1)
    m_pad = -(-m // MOE_BLOCK) * MOE_BLOCK + N_EXPERTS * MOE_BLOCK
    n_blocks = m_pad // MOE_BLOCK
    spare = (t + jnp.arange(m_pad, dtype=jnp.int32) % MOE_BLOCK) * TOP_K
    row_src = spare.at[dest].set(jnp.arange(m, dtype=jnp.int32), unique_indices=True, mode='promise_in_bounds')
    block_start = jnp.arange(n_blocks, dtype=jnp.int32) * MOE_BLOCK
    owner = jnp.sum((block_start[:, None] >= pend[None, :]).astype(jnp.int32), axis=1)
    block_expert = jnp.minimum(owner, N_EXPERTS - 1)
    n_used = (pend[-1] // MOE_BLOCK).astype(jnp.int32).reshape(1)
    return block_expert.astype(jnp.int32), row_src, n_used


def _combine_kernel(o_ref, w_ref, x_ref, gt_ref, fg_ref, out_ref, *, final, tile_off):
    is_ctx = (pl.program_id(0) + tile_off) == 0
    d = x_ref.shape[1]
    wts = w_ref[...]
    f = o_ref[:, 0:d] * wts[:, 0:1] + o_ref[:, d:2 * d] * wts[:, 1:2]
    gt = jnp.where(is_ctx, gt_ref[1:2, :], gt_ref[0:1, :])
    xn = x_ref[...] + gt * f
    if final:
        xn = _rms(xn) * fg_ref[...]
    out_ref[...] = xn


def _combine(o2, wts, xa, gt, final_g, final):
    t, d = xa.shape
    tm = TOKEN_TILE
    off = 1 if final else 0
    nt = t // tm - off
    return pl.pallas_call(
        functools.partial(_combine_kernel, final=final, tile_off=off),
        grid=(nt,),
        in_specs=[pl.BlockSpec((tm, 2 * d), lambda i: (i + off, 0)), pl.BlockSpec((tm, TOP_K), lambda i: (i + off, 0)),
                  pl.BlockSpec((tm, d), lambda i: (i + off, 0)), pl.BlockSpec((2, d), lambda i: (0, 0)),
                  pl.BlockSpec((1, d), lambda i: (0, 0))],
        out_specs=pl.BlockSpec((tm, d), lambda i: (i, 0)),
        out_shape=jax.ShapeDtypeStruct((nt * tm, d), F32),
        compiler_params=_cparams(("parallel",)),
        name="combine",
    )(o2, wts, xa, gt, final_g.reshape(1, d))


def _rope_tables(n):
    pos = jnp.arange(n)
    half = RET_HEAD // 2
    inv = ROPE_BASE ** (-jnp.arange(0, half, 2, dtype=F32) / half)

    def ang(p):
        a = p.astype(F32)[:, None] * inv
        return jnp.concatenate([a, a], axis=-1)

    a = jnp.concatenate([ang(pos // GRID_W), ang(pos % GRID_W)], axis=-1)
    sign = jnp.tile(jnp.concatenate([-jnp.ones((half // 2,), F32), jnp.ones((half // 2,), F32)]), 2)
    cos = jnp.concatenate([jnp.ones((CTX_LEN, RET_HEAD), F32), jnp.cos(a)], axis=0)
    sin = jnp.concatenate([jnp.zeros((CTX_LEN, RET_HEAD), F32), jnp.sin(a) * sign], axis=0)
    return cos, sin


def _pad_rows(w, rows, offset):
    out = jnp.zeros((rows,) + w.shape[1:], w.dtype)
    return lax.dynamic_update_slice_in_dim(out, w, offset, axis=0)


def kernel(x, c, ctx, c_ctx, ada_w, ada_b, norm1_g, norm2_g, w_in, shift_mu, rwkv_w0, rwkv_w2, rwkv_a0, rwkv_a2,
           rwkv_g2, rwkv_k_k, rwkv_k_a, rwkv_r_k, rwkv_lnx_w, rwkv_lnx_b, ret_log2_decay, ret_gn_w, w_out,
           router_w, router_b, moe_w1, moe_w2, final_g):
    b, n, d = x.shape
    assert b == 1 and d == D_MODEL and ctx.shape[1] == CTX_LEN and n % TOKEN_TILE == 0
    depth = ada_w.shape[0]
    w = RWKV_WIDTH

    xa = jnp.concatenate([ctx[0], x[0]], axis=0)
    cct = jnp.stack([c[0], c_ctx], axis=1)
    mod = _ada_mod(cct, ada_w, ada_b).reshape(depth, 2, 6, d)
    cos, sin = _rope_tables(n)
    rwt = jnp.pad(router_w, ((0, 0), (0, LANE - N_EXPERTS)))

    out = None
    for l in range(depth):
        last = l == depth - 1
        sh1, sc1, gt1, sh2, sc2, gt2 = (mod[l, :, i, :] for i in range(6))
        h = _norm_mod(xa, norm1_g[l], sc1, sh1)
        p = _matmul(h, w_in, l, RWKV_PAD, BF16)
        p_ret = _matmul(h, w_in[l:l + 1, :, RWKV_IN:], 0, 4 * RET_WIDTH, BF16)

        mu = jnp.pad(shift_mu[l], ((0, 0), (0, RWKV_PAD - RWKV_IN)))
        w2p = jnp.stack([_pad_rows(rwkv_w2[l, 0], LANE, 0), _pad_rows(rwkv_w2[l, 1], LANE, DECAY_LORA)])
        a2p = jnp.stack([_pad_rows(rwkv_a2[l, 0], LANE, 0), _pad_rows(rwkv_a2[l, 1], LANE, ICLR_LORA)])
        g2p = _pad_rows(rwkv_g2[l], 2 * LANE, 0)
        r, v, kk, g, bonus, lw, kd, bb = _rwkv_prep(
            p, mu, w2p.astype(BF16), a2p.astype(BF16), g2p.astype(BF16), rwkv_w0[l], rwkv_a0[l],
            rwkv_k_k[l].reshape(1, w), rwkv_k_a[l].reshape(1, w), rwkv_r_k[l].reshape(1, w))
        ya = _rwkv_scan(r, v, kk, lw, kd, bb)

        l2d = jnp.broadcast_to(ret_log2_decay[l].astype(F32)[:, :, None], (2, RET_HEADS, LANE))
        yb = _retention(p_ret, cos, sin, l2d, ret_gn_w[l])

        xa, h2, scores_t = _mix_out(ya, bonus, g, yb, xa, w_out[l].astype(BF16), rwkv_lnx_w[l], rwkv_lnx_b[l],
                                    gt1, norm2_g[l], sc2, sh2, rwt)
        expert, gate = _route(scores_t[:, :N_EXPERTS].T, router_b)
        block_expert, row_src, n_used = _moe_plan(expert)
        o2 = _moe_experts(h2, block_expert, row_src, n_used, moe_w1, moe_w2, l)
        res = _combine(o2, gate.T, xa, gt2, final_g, last)
        if last:
            out = res
        else:
            xa = res
    return out[None]
```
